```python
import jax, jax.numpy as jnp
from jax import lax
import numpy as np

D_MODEL = 2048
BATCH = 1
SEQ = 8192
DEPTH = 1

MLA_HEADS = 16
MLA_Q_RANK = 512
MLA_KV_RANK = 512
MLA_NOPE = 128
MLA_ROPE = 64
MLA_V = 128
MLA_QK_DIM = MLA_NOPE + MLA_ROPE
ROPE_THETA = 10000.0
FOX_HEADS = 16
FOX_HEAD_DIM = 128
FOX_FORGET_BIAS = 2.0
MLA_WIDTH = MLA_HEADS * MLA_V
FOX_WIDTH = FOX_HEADS * FOX_HEAD_DIM
MIX_WIDTH = MLA_WIDTH + FOX_WIDTH
Q_BLOCK = 128
PEER_HEADS = 8
PEER_QUERY_DIM = 256
PEER_HALF = PEER_QUERY_DIM // 2
PEER_N_KEYS = 128
PEER_N_EXPERTS = PEER_N_KEYS * PEER_N_KEYS
PEER_TOPK = 16
PEER_BLOCK = 128
NORM_EPS = 1e-6

IN_SPLIT_SIZES = (MLA_Q_RANK, MLA_KV_RANK, MLA_ROPE,
                  FOX_WIDTH, FOX_WIDTH, FOX_WIDTH, FOX_HEADS,
                  MLA_WIDTH, FOX_WIDTH)
IN_WIDTH = sum(IN_SPLIT_SIZES)
IN_SPLIT_POINTS = tuple(sum(IN_SPLIT_SIZES[:i + 1]) for i in range(len(IN_SPLIT_SIZES) - 1))

kernel_name = "hybrid_mla_fox_peer_block"


def rms_norm(x, g):
    xf = x.astype(jnp.float32)
    y = xf * lax.rsqrt(jnp.mean(xf * xf, axis=-1, keepdims=True) + NORM_EPS)
    return (y * g.astype(jnp.float32)).astype(x.dtype)


def rope(x, positions):
    half = x.shape[-1] // 2
    inv_freq = ROPE_THETA ** (-jnp.arange(half, dtype=jnp.float32) / half)
    ang = positions.astype(jnp.float32)[:, :, None, None] * inv_freq
    cos, sin = jnp.cos(ang), jnp.sin(ang)
    xf = x.astype(jnp.float32)
    x1, x2 = xf[..., :half], xf[..., half:]
    return jnp.concatenate([x1 * cos - x2 * sin, x1 * sin + x2 * cos], axis=-1).astype(x.dtype)


def causal_block_attention(q, k, v, log_forget_cum=None):
    B, S, H, Dk = q.shape
    Dv = v.shape[-1]
    nb = S // Q_BLOCK
    scale = Dk ** -0.5
    kf = k.astype(jnp.float32)
    vf = v.astype(jnp.float32)
    key_pos = jnp.arange(S, dtype=jnp.int32)
    q_blocks = q.reshape(B, nb, Q_BLOCK, H, Dk).swapaxes(0, 1)
    starts = jnp.arange(nb, dtype=jnp.int32) * Q_BLOCK
    if log_forget_cum is None:
        xs = (q_blocks, starts)
    else:
        cum_keys = log_forget_cum.astype(jnp.float32).transpose(0, 2, 1)
        cum_blocks = log_forget_cum.astype(jnp.float32).reshape(B, nb, Q_BLOCK, H).swapaxes(0, 1)
        xs = (q_blocks, starts, cum_blocks)

    def one_block(args):
        q_blk, start = args[0], args[1]
        s = jnp.einsum('bqhd,bkhd->bhqk', q_blk.astype(jnp.float32), kf) * scale
        if log_forget_cum is not None:
            cum_q = args[2].transpose(0, 2, 1)
            s = s + cum_q[..., :, None] - cum_keys[:, :, None, :]
        q_pos = start + jnp.arange(Q_BLOCK, dtype=jnp.int32)
        mask = key_pos[None, :] <= q_pos[:, None]
        s = jnp.where(mask, s, -jnp.inf)
        p = jax.nn.softmax(s, axis=-1)
        return jnp.einsum('bhqk,bkhd->bqhd', p, vf).astype(v.dtype)

    out = lax.map(one_block, xs)
    return out.swapaxes(0, 1).reshape(B, S, H, Dv)


def peer_ffn(h, w_q, sub_keys, u, v):
    B, S, D = h.shape
    T = B * S
    ht = h.reshape(T, D)
    q = jnp.einsum('td,de->te', ht, w_q).reshape(T, PEER_HEADS, 2, PEER_HALF).astype(jnp.float32)
    scores = jnp.einsum('thcd,hcnd->thcn', q, sub_keys.astype(jnp.float32))
    s1, i1 = lax.top_k(scores[:, :, 0], PEER_TOPK)
    s2, i2 = lax.top_k(scores[:, :, 1], PEER_TOPK)
    cand_scores = (s1[..., :, None] + s2[..., None, :]).reshape(T, PEER_HEADS, PEER_TOPK * PEER_TOPK)
    cand_idx = (i1[..., :, None] * PEER_N_KEYS + i2[..., None, :]).reshape(T, PEER_HEADS, PEER_TOPK * PEER_TOPK)
    top_scores, top_pos = lax.top_k(cand_scores, PEER_TOPK)
    experts = jnp.take_along_axis(cand_idx, top_pos, axis=-1)
    gates = jax.nn.softmax(top_scores, axis=-1)

    nb = T // PEER_BLOCK
    xs = (ht.reshape(nb, PEER_BLOCK, D),
          experts.reshape(nb, PEER_BLOCK, PEER_HEADS, PEER_TOPK),
          gates.reshape(nb, PEER_BLOCK, PEER_HEADS, PEER_TOPK))

    def apply_block(args):
        xb, eb, gb = args
        ub = u[eb].astype(jnp.float32)
        a = jnp.einsum('td,thkd->thk', xb.astype(jnp.float32), ub)
        w = gb * jax.nn.gelu(a, approximate=False)
        vb = v[eb].astype(jnp.float32)
        return jnp.einsum('thk,thkd->td', w, vb).astype(h.dtype)

    out = lax.map(apply_block, xs)
    return out.reshape(B, S, D)


def setup_inputs(seed: int = 0) -> dict:
    key = jax.random.key(seed)
    ks = jax.random.split(key, 24)
    f32 = jnp.float32

    def normal(k, shape, scale):
        return jax.random.normal(k, shape, f32) * scale

    def gain(k, shape):
        return 1.0 + 0.02 * jax.random.normal(k, shape, f32)

    L = DEPTH
    return {
        "x": jax.random.normal(ks[0], (BATCH, SEQ, D_MODEL), f32),
        "positions": jnp.broadcast_to(jnp.arange(SEQ, dtype=jnp.int32)[None, :], (BATCH, SEQ)),
        "mix_norm_g": gain(ks[1], (L, D_MODEL)),
        "w_in": normal(ks[2], (L, D_MODEL, IN_WIDTH), D_MODEL ** -0.5),
        "b_forget": FOX_FORGET_BIAS + 0.1 * jax.random.normal(ks[3], (L, FOX_HEADS), f32),
        "b_gate": normal(ks[4], (L, MIX_WIDTH), 0.02),
        "mla_q_latent_g": gain(ks[5], (L, MLA_Q_RANK)),
        "w_q_up": normal(ks[6], (L, MLA_Q_RANK, MLA_HEADS * MLA_QK_DIM), MLA_Q_RANK ** -0.5),
        "mla_kv_latent_g": gain(ks[7], (L, MLA_KV_RANK)),
        "w_kv_up": normal(ks[8], (L, MLA_KV_RANK, MLA_HEADS * (MLA_NOPE + MLA_V)), MLA_KV_RANK ** -0.5),
        "mla_q_norm_g": gain(ks[9], (L, MLA_QK_DIM)),
        "mla_k_norm_g": gain(ks[10], (L, MLA_QK_DIM)),
        "fox_q_norm_g": gain(ks[11], (L, FOX_HEAD_DIM)),
        "fox_k_norm_g": gain(ks[12], (L, FOX_HEAD_DIM)),
        "w_out": normal(ks[13], (L, MIX_WIDTH, D_MODEL), MIX_WIDTH ** -0.5),
        "ffn_norm_g": gain(ks[14], (L, D_MODEL)),
        "w_peer_q": normal(ks[15], (L, D_MODEL, PEER_HEADS * PEER_QUERY_DIM), D_MODEL ** -0.5),
        "peer_sub_keys": normal(ks[16], (L, PEER_HEADS, 2, PEER_N_KEYS, PEER_HALF), PEER_HALF ** -0.5),
        "peer_u": normal(ks[17], (L, PEER_N_EXPERTS, D_MODEL), D_MODEL ** -0.5),
        "peer_v": normal(ks[18], (L, PEER_N_EXPERTS, D_MODEL), PEER_HEADS ** -0.5),
    }


def reference(x, positions, mix_norm_g, w_in, b_forget, b_gate, mla_q_latent_g, w_q_up,
              mla_kv_latent_g, w_kv_up, mla_q_norm_g, mla_k_norm_g, fox_q_norm_g, fox_k_norm_g,
              w_out, ffn_norm_g, w_peer_q, peer_sub_keys, peer_u, peer_v):
    B, S, _ = x.shape
    for layer in range(DEPTH):
        h = rms_norm(x, mix_norm_g[layer])
        proj = jnp.einsum('bsd,de->bse', h, w_in[layer])
        (c_q, c_kv, k_pe, fq, fk, fv, f_logit, gate_a, gate_b) = jnp.split(proj, IN_SPLIT_POINTS, axis=-1)

        q_a = jnp.einsum('bsr,re->bse', rms_norm(c_q, mla_q_latent_g[layer]), w_q_up[layer])
        q_a = q_a.reshape(B, S, MLA_HEADS, MLA_QK_DIM)
        kv = jnp.einsum('bsr,re->bse', rms_norm(c_kv, mla_kv_latent_g[layer]), w_kv_up[layer])
        kv = kv.reshape(B, S, MLA_HEADS, MLA_NOPE + MLA_V)
        k_nope, v_a = kv[..., :MLA_NOPE], kv[..., MLA_NOPE:]
        k_pe_h = jnp.broadcast_to(k_pe[:, :, None, :], (B, S, MLA_HEADS, MLA_ROPE))
        k_a = jnp.concatenate([k_nope, k_pe_h], axis=-1)
        q_a = rms_norm(q_a, mla_q_norm_g[layer])
        k_a = rms_norm(k_a, mla_k_norm_g[layer])
        q_a = jnp.concatenate([q_a[..., :MLA_NOPE], rope(q_a[..., MLA_NOPE:], positions)], axis=-1)
        k_a = jnp.concatenate([k_a[..., :MLA_NOPE], rope(k_a[..., MLA_NOPE:], positions)], axis=-1)
        y_a = causal_block_attention(q_a, k_a, v_a).reshape(B, S, MLA_WIDTH)

        q_b = rms_norm(fq.reshape(B, S, FOX_HEADS, FOX_HEAD_DIM), fox_q_norm_g[layer])
        k_b = rms_norm(fk.reshape(B, S, FOX_HEADS, FOX_HEAD_DIM), fox_k_norm_g[layer])
        v_b = fv.reshape(B, S, FOX_HEADS, FOX_HEAD_DIM)
        log_f = jax.nn.log_sigmoid((f_logit + b_forget[layer]).astype(jnp.float32))
        cum_log_f = jnp.cumsum(log_f, axis=1)
        y_b = causal_block_attention(q_b, k_b, v_b, cum_log_f).reshape(B, S, FOX_WIDTH)

        gates = jax.nn.sigmoid(jnp.concatenate([gate_a, gate_b], axis=-1) + b_gate[layer])
        merged = jnp.concatenate([y_a, y_b], axis=-1) * gates
        x = x + jnp.einsum('bse,ed->bsd', merged, w_out[layer])

        h2 = rms_norm(x, ffn_norm_g[layer])
        x = x + peer_ffn(h2, w_peer_q[layer], peer_sub_keys[layer], peer_u[layer], peer_v[layer])
    return x
```

```python
import functools
import math

import jax
import jax.numpy as jnp
from jax import lax
from jax.experimental import pallas as pl
from jax.experimental.pallas import tpu as pltpu

F32 = jnp.float32
BF16 = jnp.bfloat16

LANES = 128
NORM_EPS = 1e-6
ROPE_THETA = 10000.0
MLA_HEADS = 16
MLA_Q_RANK = 512
MLA_KV_RANK = 512
MLA_NOPE = 128
MLA_ROPE = 64
MLA_V = 128
MLA_QK_DIM = MLA_NOPE + MLA_ROPE
MLA_QK_PAD = 2 * LANES
FOX_HEADS = 16
FOX_HEAD_DIM = 128
PEER_HEADS = 8
PEER_HALF = 128
PEER_N_KEYS = 128
PEER_TOPK = 16
PEER_SORT = PEER_TOPK + 1
VMEM_LIMIT = 56 * 1024 * 1024


def _cparams(sem):
    return pltpu.CompilerParams(dimension_semantics=sem, vmem_limit_bytes=VMEM_LIMIT)


def _proj_kernel(x_ref, g_ref, w_ref, gain_ref, o_ref, hn_ref, *, n_norm_tiles):
    j = pl.program_id(1)

    @pl.when(j == 0)
    def _():
        x = x_ref[...]
        ms = jnp.mean(x * x, axis=-1, keepdims=True)
        hn_ref[...] = (x * lax.rsqrt(ms + NORM_EPS) * g_ref[...]).astype(BF16)

    acc = jnp.dot(hn_ref[...], w_ref[...], preferred_element_type=F32)
    tn = acc.shape[1]

    @pl.when(j < n_norm_tiles)
    def _():
        for c in range(tn // LANES):
            sl = slice(c * LANES, (c + 1) * LANES)
            a = acc[:, sl]
            ms = jnp.mean(a * a, axis=-1, keepdims=True)
            o_ref[:, sl] = (a * lax.rsqrt(ms + NORM_EPS) * gain_ref[:, sl]).astype(o_ref.dtype)

    @pl.when(j >= n_norm_tiles)
    def _():
        o_ref[...] = acc.astype(o_ref.dtype)


def _proj(x, g, w, gain, *, tm, tn, n_norm_tiles, out_dtype):
    s, d = x.shape
    n = w.shape[1]
    return pl.pallas_call(
        functools.partial(_proj_kernel, n_norm_tiles=n_norm_tiles),
        grid=(s // tm, n // tn),
        in_specs=[
            pl.BlockSpec((tm, d), lambda i, j: (i, 0)),
            pl.BlockSpec((1, d), lambda i, j: (0, 0)),
            pl.BlockSpec((d, tn), lambda i, j: (0, j)),
            pl.BlockSpec((1, tn), lambda i, j: (0, j)),
        ],
        out_specs=pl.BlockSpec((tm, tn), lambda i, j: (i, j)),
        out_shape=jax.ShapeDtypeStruct((s, n), out_dtype),
        scratch_shapes=[pltpu.VMEM((tm, d), BF16)],
        compiler_params=_cparams(("parallel", "arbitrary")),
        name="proj",
    )(x, g, w, gain)


def _log_sigmoid(z):
    return jnp.minimum(z, 0.0) - jnp.log1p(jnp.exp(-jnp.abs(z)))


def _cum_kernel(tail_ref, b_ref, o_ref, *, rows):
    s = tail_ref.shape[0]
    r_i = lax.broadcasted_iota(jnp.int32, (rows, rows), 0)
    c_i = lax.broadcasted_iota(jnp.int32, (rows, rows), 1)
    tri = (c_i <= r_i).astype(F32)

    def body(i, carry):
        start = pl.multiple_of(i * rows, rows)
        lf = _log_sigmoid(tail_ref[pl.ds(start, rows), :] + b_ref[...])
        c = jnp.dot(tri, lf, preferred_element_type=F32, precision=lax.Precision.HIGHEST) + carry
        o_ref[:, pl.ds(start, rows)] = c.T[:FOX_HEADS, :]
        return c[rows - 1:rows, :]

    lax.fori_loop(0, s // rows, body, jnp.zeros((1, LANES), F32))


def _cum(tail, b_pad, *, rows):
    s = tail.shape[0]
    return pl.pallas_call(
        functools.partial(_cum_kernel, rows=rows),
        grid=(1,),
        in_specs=[pl.BlockSpec((s, LANES), lambda i: (0, 0)), pl.BlockSpec((1, LANES), lambda i: (0, 0))],
        out_specs=pl.BlockSpec((FOX_HEADS, s), lambda i: (0, 0)),
        out_shape=jax.ShapeDtypeStruct((FOX_HEADS, s), F32),
        compiler_params=_cparams(("arbitrary",)),
        name="cum",
    )(tail, b_pad)


def _rope_tile(v, cos, sin_signed):
    return v * cos + pltpu.roll(v, LANES // 2, 1) * sin_signed


def _mla_prep_kernel(cq_ref, ckv_ref, tail_ref, ang_ref, gql_ref, gkvl_ref, wq_ref, wk_ref, wv_ref,
                     gq_ref, gk_ref, q_ref, k_ref, v_ref, cqn_ref, ckvn_ref, kpe_ref, cos_ref, sin_ref,
                     *, q_scale):
    h = pl.program_id(1)

    @pl.when(h == 0)
    def _():
        cq = cq_ref[...].astype(F32)
        ms = jnp.mean(cq * cq, axis=-1, keepdims=True)
        cqn_ref[...] = (cq * lax.rsqrt(ms + NORM_EPS) * gql_ref[...]).astype(BF16)
        ckv = ckv_ref[...].astype(F32)
        ms = jnp.mean(ckv * ckv, axis=-1, keepdims=True)
        ckvn_ref[...] = (ckv * lax.rsqrt(ms + NORM_EPS) * gkvl_ref[...]).astype(BF16)
        lane = lax.broadcasted_iota(jnp.int32, tail_ref.shape, 1)
        is_x1 = (lane >= 32) & (lane < 64)
        is_x2 = lane >= 96
        kpe_ref[...] = jnp.where(is_x1 | is_x2, tail_ref[...], 0.0)
        ang = ang_ref[...]
        cos_ref[...] = jnp.cos(ang)
        sn = jnp.sin(ang)
        sin_ref[...] = jnp.where(is_x1, -sn, sn)

    cos = cos_ref[...]
    sin_signed = sin_ref[...]

    q = jnp.dot(cqn_ref[...], wq_ref[...], preferred_element_type=F32)
    ms = jnp.sum(q * q, axis=-1, keepdims=True) * (1.0 / MLA_QK_DIM)
    qn = q * (lax.rsqrt(ms + NORM_EPS) * q_scale) * gq_ref[...]
    q_ref[:, :LANES] = qn[:, :LANES].astype(BF16)
    q_ref[:, LANES:] = _rope_tile(qn[:, LANES:], cos, sin_signed).astype(BF16)

    kn = jnp.dot(ckvn_ref[...], wk_ref[...], preferred_element_type=F32)
    kpe = kpe_ref[...]
    ss = jnp.sum(kn * kn, axis=-1, keepdims=True) + jnp.sum(kpe * kpe, axis=-1, keepdims=True)
    r = lax.rsqrt(ss * (1.0 / MLA_QK_DIM) + NORM_EPS)
    k_ref[:, :LANES] = (kn * r * gk_ref[:, :LANES]).astype(BF16)
    k_ref[:, LANES:] = _rope_tile(kpe * r * gk_ref[:, LANES:], cos, sin_signed).astype(BF16)

    v_ref[...] = jnp.dot(ckvn_ref[...], wv_ref[...], preferred_element_type=F32).astype(BF16)


def _mla_prep(main, tail, ang, gql, gkvl, wq, wk, wv, gq, gk, *, tm, cq_blk, ckv_blk):
    s = main.shape[0]
    heads = MLA_HEADS
    return pl.pallas_call(
        functools.partial(_mla_prep_kernel, q_scale=MLA_QK_DIM ** -0.5),
        grid=(s // tm, heads),
        in_specs=[
            pl.BlockSpec((tm, MLA_Q_RANK), lambda i, h: (i, cq_blk)),
            pl.BlockSpec((tm, MLA_KV_RANK), lambda i, h: (i, ckv_blk)),
            pl.BlockSpec((tm, LANES), lambda i, h: (i, 0)),
            pl.BlockSpec((tm, LANES), lambda i, h: (i, 0)),
            pl.BlockSpec((1, MLA_Q_RANK), lambda i, h: (0, 0)),
            pl.BlockSpec((1, MLA_KV_RANK), lambda i, h: (0, 0)),
            pl.BlockSpec((MLA_Q_RANK, MLA_QK_PAD), lambda i, h: (0, h)),
            pl.BlockSpec((MLA_KV_RANK, MLA_NOPE), lambda i, h: (0, h)),
            pl.BlockSpec((MLA_KV_RANK, MLA_V), lambda i, h: (0, h)),
            pl.BlockSpec((1, MLA_QK_PAD), lambda i, h: (0, 0)),
            pl.BlockSpec((1, MLA_QK_PAD), lambda i, h: (0, 0)),
        ],
        out_specs=[
            pl.BlockSpec((tm, MLA_QK_PAD), lambda i, h: (i, h)),
            pl.BlockSpec((tm, MLA_QK_PAD), lambda i, h: (i, h)),
            pl.BlockSpec((tm, MLA_V), lambda i, h: (i, h)),
        ],
        out_shape=[
            jax.ShapeDtypeStruct((s, heads * MLA_QK_PAD), BF16),
            jax.ShapeDtypeStruct((s, heads * MLA_QK_PAD), BF16),
            jax.ShapeDtypeStruct((s, heads * MLA_V), BF16),
        ],
        scratch_shapes=[
            pltpu.VMEM((tm, MLA_Q_RANK), BF16),
            pltpu.VMEM((tm, MLA_KV_RANK), BF16),
            pltpu.VMEM((tm, LANES), F32),
            pltpu.VMEM((tm, LANES), F32),
            pltpu.VMEM((tm, LANES), F32),
        ],
        compiler_params=_cparams(("parallel", "arbitrary")),
        name="mla_prep",
    )(main, main, tail, ang, gql, gkvl, wq, wk, wv, gq, gk)


def _attn_kernel(*refs, tq, has_cum):
    if has_cum:
        q_ref, k_ref, v_ref, cum_ref, o_ref = refs
    else:
        q_ref, k_ref, v_ref, o_ref = refs
    qi = pl.program_id(1)
    q = q_ref[...]
    dv = v_ref.shape[1]

    def chunk(j, carry, masked):
        m, l, acc = carry
        start = pl.multiple_of(j * tq, tq)
        k = k_ref[pl.ds(start, tq), :]
        s = lax.dot_general(q, k, (((1,), (1,)), ((), ())), preferred_element_type=F32)
        if has_cum:
            s = s - cum_ref[:, pl.ds(start, tq)]
        if masked:
            row = lax.broadcasted_iota(jnp.int32, (tq, tq), 0)
            col = lax.broadcasted_iota(jnp.int32, (tq, tq), 1)
            s = jnp.where(col <= row, s, -jnp.inf)
        m_new = jnp.maximum(m, jnp.max(s, axis=-1, keepdims=True))
        alpha = jnp.exp(m - m_new)
        p = jnp.exp(s - m_new)
        l_new = alpha * l + jnp.sum(p, axis=-1, keepdims=True)
        pv = jnp.dot(p.astype(BF16), v_ref[pl.ds(start, tq), :], preferred_element_type=F32)
        return m_new, l_new, alpha * acc + pv

    init = (jnp.full((tq, 1), -jnp.inf, F32), jnp.zeros((tq, 1), F32), jnp.zeros((tq, dv), F32))
    carry = lax.fori_loop(0, qi, lambda j, c: chunk(j, c, False), init)
    _, l, acc = chunk(qi, carry, True)
    o_ref[...] = (acc / l).astype(o_ref.dtype)


def _attention(q, k, v, cum, *, heads, dq, dv, q_off, k_off, v_off, tq):
    s = q.shape[0]
    in_specs = [
        pl.BlockSpec((tq, dq), lambda h, i: (i, q_off + h)),
        pl.BlockSpec((s, dq), lambda h, i: (0, k_off + h)),
        pl.BlockSpec((s, dv), lambda h, i: (0, v_off + h)),
    ]
    args = [q, k, v]
    if cum is not None:
        in_specs.append(pl.BlockSpec((None, 1, s), lambda h, i: (h, 0, 0)))
        args.append(cum)
    return pl.pallas_call(
        functools.partial(_attn_kernel, tq=tq, has_cum=cum is not None),
        grid=(heads, s // tq),
        in_specs=in_specs,
        out_specs=pl.BlockSpec((tq, dv), lambda h, i: (i, h)),
        out_shape=jax.ShapeDtypeStruct((s, heads * dv), BF16),
        compiler_params=_cparams(("parallel", "arbitrary")),
        name="attn_fox" if cum is not None else "attn_mla",
    )(*args)


def _merge_kernel(ya_ref, yb_ref, ga_ref, gb_ref, bg_ref, w_ref, x_ref, o_ref):
    half = ya_ref.shape[1]
    ga = ga_ref[...].astype(F32) + bg_ref[:, :half]
    gb = gb_ref[...].astype(F32) + bg_ref[:, half:]
    ma = (ya_ref[...].astype(F32) / (1.0 + jnp.exp(-ga))).astype(BF16)
    mb = (yb_ref[...].astype(F32) / (1.0 + jnp.exp(-gb))).astype(BF16)
    acc = jnp.dot(ma, w_ref[:half, :], preferred_element_type=F32)
    acc = acc + jnp.dot(mb, w_ref[half:, :], preferred_element_type=F32)
    o_ref[...] = x_ref[...] + acc


def _merge(ya, yb, main, b_gate, w_out, x, *, tm, ga_blk, gb_blk):
    s, d = x.shape
    half = ya.shape[1]
    return pl.pallas_call(
        _merge_kernel,
        grid=(s // tm,),
        in_specs=[
            pl.BlockSpec((tm, half), lambda i: (i, 0)),
            pl.BlockSpec((tm, half), lambda i: (i, 0)),
            pl.BlockSpec((tm, half), lambda i: (i, ga_blk)),
            pl.BlockSpec((tm, half), lambda i: (i, gb_blk)),
            pl.BlockSpec((1, 2 * half), lambda i: (0, 0)),
            pl.BlockSpec((2 * half, d), lambda i: (0, 0), pipeline_mode=pl.Buffered(1)),
            pl.BlockSpec((tm, d), lambda i: (i, 0)),
        ],
        out_specs=pl.BlockSpec((tm, d), lambda i: (i, 0)),
        out_shape=jax.ShapeDtypeStruct((s, d), F32),
        compiler_params=_cparams(("parallel",)),
        name="merge",
    )(ya, yb, main, main, b_gate, w_out, x)


def _pairs():
    return [(p, q) for p in range(PEER_SORT) for q in range(PEER_SORT) if (p + 1) * (q + 1) <= PEER_SORT]


def _route_kernel(x_ref, g_ref, wq_ref, keys_ref, h2_ref, s2_ref, e2_ref, thr_ref, e1n_ref, a_ref, b_ref, s1_ref):
    x = x_ref[...]
    ms = jnp.mean(x * x, axis=-1, keepdims=True)
    h2 = (x * lax.rsqrt(ms + NORM_EPS) * g_ref[...]).astype(BF16)
    h2_ref[...] = h2
    qp = jnp.dot(h2, wq_ref[...], preferred_element_type=F32)
    neg_inf = -jnp.inf

    def scores_t(h, c):
        blk = qp[:, (2 * h + c) * PEER_HALF:(2 * h + c + 1) * PEER_HALF]
        return lax.dot_general(keys_ref[2 * h + c], blk, (((1,), (1,)), ((), ())),
                               preferred_element_type=F32, precision=lax.Precision.HIGHEST)

    def top_values(st, dst_ref, h):
        work = st
        for r in range(PEER_SORT):
            m = jnp.max(work, axis=0, keepdims=True)
            dst_ref[r, h:h + 1, :] = m
            if r + 1 < PEER_SORT:
                work = jnp.where(work == m, neg_inf, work)

    for h in range(PEER_HEADS):
        s1 = scores_t(h, 0)
        s2 = scores_t(h, 1)
        s1_ref[h] = s1
        s2_ref[h] = s2
        top_values(s1, a_ref, h)
        top_values(s2, b_ref, h)

    cands = [a_ref[p] + b_ref[q] for (p, q) in _pairs()]
    work = list(cands)
    tau = None
    for r in range(PEER_SORT):
        m = functools.reduce(jnp.maximum, work)
        if r + 1 == PEER_TOPK:
            tau = m
        if r + 1 < PEER_SORT:
            work = [jnp.where(c == m, neg_inf, c) for c in work]
    tau_mid = 0.5 * (tau + m)
    top = a_ref[0] + b_ref[0]
    z = functools.reduce(lambda u, w: u + w, [jnp.where(c >= tau, jnp.exp(c - top), 0.0) for c in cands])
    inv_z = 1.0 / z
    a0 = a_ref[0]
    b0 = b_ref[0]
    for h in range(PEER_HEADS):
        s1 = s1_ref[h]
        thr_ref[:, h, :] = tau_mid[h:h + 1, :] - s1
        e1n_ref[:, h, :] = jnp.exp(s1 - a0[h:h + 1, :]) * inv_z[h:h + 1, :]
        e2_ref[h] = jnp.exp(s2_ref[h] - b0[h:h + 1, :])


def _route(x1, g, wq, keys, *, tm):
    s, d = x1.shape
    route_shape = jax.ShapeDtypeStruct((PEER_HEADS, PEER_N_KEYS, s), F32)
    route_spec = pl.BlockSpec((PEER_HEADS, PEER_N_KEYS, tm), lambda i: (0, 0, i))
    by_key_shape = jax.ShapeDtypeStruct((PEER_N_KEYS, PEER_HEADS, s), F32)
    by_key_spec = pl.BlockSpec((PEER_N_KEYS, PEER_HEADS, tm), lambda i: (0, 0, i))
    return pl.pallas_call(
        _route_kernel,
        grid=(s // tm,),
        in_specs=[
            pl.BlockSpec((tm, d), lambda i: (i, 0)),
            pl.BlockSpec((1, d), lambda i: (0, 0)),
            pl.BlockSpec(wq.shape, lambda i: (0, 0), pipeline_mode=pl.Buffered(1)),
            pl.BlockSpec(keys.shape, lambda i: (0, 0, 0)),
        ],
        out_specs=[pl.BlockSpec((tm, d), lambda i: (i, 0)), route_spec, route_spec, by_key_spec, by_key_spec],
        out_shape=[jax.ShapeDtypeStruct((s, d), BF16), route_shape, route_shape, by_key_shape, by_key_shape],
        scratch_shapes=[pltpu.VMEM((PEER_SORT, PEER_HEADS, tm), F32), pltpu.VMEM((PEER_SORT, PEER_HEADS, tm), F32),
                        pltpu.VMEM((PEER_HEADS, PEER_N_KEYS, tm), F32)],
        compiler_params=_cparams(("parallel",)),
        name="route",
    )(x1, g, wq, keys)


def _gelu(a):
    return 0.5 * a * (1.0 + lax.erf(a * (2.0 ** -0.5)))


def _peer_kernel(h2_ref, u_ref, v_ref, s2_ref, e2_ref, thr_ref, e1n_ref, x1_ref, o_ref, w_ref):
    ei = pl.program_id(1)
    te = u_ref.shape[0]
    tb = h2_ref.shape[0]
    ni = te // PEER_N_KEYS

    @pl.when(ei == 0)
    def _():
        o_ref[...] = x1_ref[...]

    a_t = lax.dot_general(u_ref[...], h2_ref[...], (((1,), (1,)), ((), ())), preferred_element_type=F32)
    for ii in range(ni):
        for tt in range(tb // LANES):
            tok = slice(tt * LANES, (tt + 1) * LANES)
            g = jnp.zeros((PEER_N_KEYS, LANES), F32)
            for h in range(PEER_HEADS):
                thr = thr_ref[ii, h:h + 1, tok]
                e1n = e1n_ref[ii, h:h + 1, tok]
                g = g + jnp.where(s2_ref[h, :, tok] >= thr, e2_ref[h, :, tok], 0.0) * e1n
            w = g * _gelu(a_t[ii * PEER_N_KEYS:(ii + 1) * PEER_N_KEYS, tok])
            w_ref[tok, ii * PEER_N_KEYS:(ii + 1) * PEER_N_KEYS] = w.T.astype(BF16)
    o_ref[...] += jnp.dot(w_ref[...], v_ref[...], preferred_element_type=F32)


def _peer(h2, u, v, s2, e2, thr, e1n, x1, *, tb, te):
    s, d = x1.shape
    ne = u.shape[0]
    route_spec = pl.BlockSpec((PEER_HEADS, PEER_N_KEYS, tb), lambda i, e: (0, 0, i))
    by_key_spec = pl.BlockSpec((te // PEER_N_KEYS, PEER_HEADS, tb), lambda i, e: (e, 0, i))
    return pl.pallas_call(
        _peer_kernel,
        grid=(s // tb, ne // te),
        in_specs=[
            pl.BlockSpec((tb, d), lambda i, e: (i, 0)),
            pl.BlockSpec((te, d), lambda i, e: (e, 0)),
            pl.BlockSpec((te, d), lambda i, e: (e, 0)),
            route_spec, route_spec, by_key_spec, by_key_spec,
            pl.BlockSpec((tb, d), lambda i, e: (i, 0)),
        ],
        out_specs=pl.BlockSpec((tb, d), lambda i, e: (i, 0)),
        out_shape=jax.ShapeDtypeStruct((s, d), F32),
        scratch_shapes=[pltpu.VMEM((tb, te), BF16)],
        compiler_params=_cparams(("parallel", "arbitrary")),
        name="peer",
    )(h2, u, v, s2, e2, thr, e1n, x1)


def _blocks(s):
    return dict(
        proj_tm=min(1024, s), proj_tn=1024,
        cum_rows=min(256, s),
        prep_tm=min(1024, s),
        attn_tq=min(512, s),
        merge_tm=min(256, s),
        route_tm=min(512, s),
        peer_tb=min(512, s), peer_te=512,
    )


def kernel(x, positions, mix_norm_g, w_in, b_forget, b_gate, mla_q_latent_g, w_q_up, mla_kv_latent_g, w_kv_up,
           mla_q_norm_g, mla_k_norm_g, fox_q_norm_g, fox_k_norm_g, w_out, ffn_norm_g, w_peer_q, peer_sub_keys,
           peer_u, peer_v):
    bsz, s, d = x.shape
    assert bsz == 1 and mix_norm_g.shape[0] == 1
    blk = _blocks(s)
    half_r = MLA_ROPE // 2
    fox_w = FOX_HEADS * FOX_HEAD_DIM
    mla_w = MLA_HEADS * MLA_V

    wi = w_in[0]
    o = 0
    parts = {}
    for name, width in (("cq", MLA_Q_RANK), ("ckv", MLA_KV_RANK), ("kpe", MLA_ROPE), ("fq", fox_w), ("fk", fox_w),
                        ("fv", fox_w), ("fl", FOX_HEADS), ("ga", mla_w), ("gb", fox_w)):
        parts[name] = wi[:, o:o + width]
        o += width
    w_main = jnp.concatenate([parts[n] for n in ("fq", "fk", "fv", "ga", "gb", "cq", "ckv")], axis=1).astype(BF16)
    zcol = lambda n: jnp.zeros((d, n), F32)
    w_tail = jnp.concatenate([parts["fl"], zcol(16), parts["kpe"][:, :half_r], zcol(32), parts["kpe"][:, half_r:]],
                             axis=1).astype(BF16)
    n_main = w_main.shape[1]
    fox_scale = FOX_HEAD_DIM ** -0.5
    gain_main = jnp.concatenate([jnp.tile(fox_q_norm_g[0], FOX_HEADS) * fox_scale, jnp.tile(fox_k_norm_g[0], FOX_HEADS),
                                 jnp.ones((n_main - 2 * fox_w,), F32)])[None, :]
    x2 = x[0]
    g_mix = mix_norm_g[0][None, :]

    tn = blk["proj_tn"]
    main = _proj(x2, g_mix, w_main, gain_main, tm=blk["proj_tm"], tn=tn, n_norm_tiles=2 * fox_w // tn, out_dtype=BF16)
    tail = _proj(x2, g_mix, w_tail, jnp.ones((1, LANES), F32), tm=blk["proj_tm"], tn=LANES, n_norm_tiles=0,
                 out_dtype=F32)

    b_pad = jnp.concatenate([b_forget[0], jnp.zeros((LANES - FOX_HEADS,), F32)])[None, :]
    cum = _cum(tail, b_pad, rows=blk["cum_rows"]).reshape(FOX_HEADS, 1, s)

    def lane_layout(a, b):
        z = jnp.zeros_like(a)
        return jnp.concatenate([z, a, z, b], axis=-1)

    wq3 = w_q_up[0].reshape(MLA_Q_RANK, MLA_HEADS, MLA_QK_DIM)
    wq_pad = jnp.concatenate([wq3[..., :MLA_NOPE],
                              lane_layout(wq3[..., MLA_NOPE:MLA_NOPE + half_r], wq3[..., MLA_NOPE + half_r:])],
                             axis=-1).reshape(MLA_Q_RANK, MLA_HEADS * MLA_QK_PAD).astype(BF16)
    wkv3 = w_kv_up[0].reshape(MLA_KV_RANK, MLA_HEADS, MLA_NOPE + MLA_V)
    wk = wkv3[..., :MLA_NOPE].reshape(MLA_KV_RANK, MLA_HEADS * MLA_NOPE).astype(BF16)
    wv = wkv3[..., MLA_NOPE:].reshape(MLA_KV_RANK, MLA_HEADS * MLA_V).astype(BF16)

    def gain_layout(gv):
        return jnp.concatenate([gv[:MLA_NOPE], lane_layout(gv[MLA_NOPE:MLA_NOPE + half_r], gv[MLA_NOPE + half_r:])])[None, :]

    inv_freq = ROPE_THETA ** (-jnp.arange(half_r, dtype=F32) / half_r)
    ang = positions[0].astype(F32)[:, None] * lane_layout(inv_freq, inv_freq)[None, :]
    q_a, k_a, v_a = _mla_prep(main, tail, ang, mla_q_latent_g[0][None, :], mla_kv_latent_g[0][None, :], wq_pad, wk, wv,
                              gain_layout(mla_q_norm_g[0]), gain_layout(mla_k_norm_g[0]), tm=blk["prep_tm"],
                              cq_blk=(n_main - MLA_Q_RANK - MLA_KV_RANK) // MLA_Q_RANK,
                              ckv_blk=(n_main - MLA_KV_RANK) // MLA_KV_RANK)

    y_a = _attention(q_a, k_a, v_a, None, heads=MLA_HEADS, dq=MLA_QK_PAD, dv=MLA_V, q_off=0, k_off=0, v_off=0,
                     tq=blk["attn_tq"])
    y_b = _attention(main, main, main, cum, heads=FOX_HEADS, dq=FOX_HEAD_DIM, dv=FOX_HEAD_DIM, q_off=0,
                     k_off=FOX_HEADS, v_off=2 * FOX_HEADS, tq=blk["attn_tq"])

    x1 = _merge(y_a, y_b, main, b_gate[0][None, :], w_out[0].astype(BF16), x2, tm=blk["merge_tm"],
                ga_blk=3 * fox_w // mla_w, gb_blk=(3 * fox_w + mla_w) // fox_w)

    keys = peer_sub_keys[0].reshape(PEER_HEADS * 2, PEER_N_KEYS, PEER_HALF)
    h2, s2, e2, thr, e1n = _route(x1, ffn_norm_g[0][None, :], w_peer_q[0].astype(BF16), keys, tm=blk["route_tm"])
    out = _peer(h2, peer_u[0].astype(BF16), peer_v[0].astype(BF16), s2, e2, thr, e1n, x1, tb=blk["peer_tb"],
                te=blk["peer_te"])
    return out[None]
```

```python
import functools
import math

import jax
import jax.numpy as jnp
from jax import lax
from jax.experimental import pallas as pl
from jax.experimental.pallas import tpu as pltpu

F32 = jnp.float32
BF16 = jnp.bfloat16

LANES = 128
LOG2E = math.log2(math.e)
NORM_EPS = 1e-6
ROPE_THETA = 10000.0
MLA_HEADS = 16
MLA_Q_RANK = 512
MLA_KV_RANK = 512
MLA_NOPE = 128
MLA_ROPE = 64
MLA_V = 128
MLA_QK_DIM = MLA_NOPE + MLA_ROPE
MLA_QK_PAD = 2 * LANES
FOX_HEADS = 16
FOX_HEAD_DIM = 128
PEER_HEADS = 8
PEER_HALF = 128
PEER_N_KEYS = 128
PEER_TOPK = 16
PEER_SORT = PEER_TOPK + 1
VMEM_LIMIT = 56 * 1024 * 1024


def _cparams(sem):
    return pltpu.CompilerParams(dimension_semantics=sem, vmem_limit_bytes=VMEM_LIMIT)


def _proj_kernel(x_ref, g_ref, w_ref, gain_ref, o_ref, hn_ref, *, n_norm_tiles):
    j = pl.program_id(1)

    @pl.when(j == 0)
    def _():
        x = x_ref[...]
        ms = jnp.mean(x * x, axis=-1, keepdims=True)
        hn_ref[...] = (x * lax.rsqrt(ms + NORM_EPS) * g_ref[...]).astype(BF16)

    acc = jnp.dot(hn_ref[...], w_ref[...], preferred_element_type=F32)
    tn = acc.shape[1]

    @pl.when(j < n_norm_tiles)
    def _():
        for c in range(tn // LANES):
            sl = slice(c * LANES, (c + 1) * LANES)
            a = acc[:, sl]
            ms = jnp.mean(a * a, axis=-1, keepdims=True)
            o_ref[:, sl] = (a * lax.rsqrt(ms + NORM_EPS) * gain_ref[:, sl]).astype(o_ref.dtype)

    @pl.when(j >= n_norm_tiles)
    def _():
        o_ref[...] = acc.astype(o_ref.dtype)


def _proj(x, g, w, gain, *, tm, tn, n_norm_tiles, out_dtype):
    s, d = x.shape
    n = w.shape[1]
    return pl.pallas_call(
        functools.partial(_proj_kernel, n_norm_tiles=n_norm_tiles),
        grid=(s // tm, n // tn),
        in_specs=[
            pl.BlockSpec((tm, d), lambda i, j: (i, 0)),
            pl.BlockSpec((1, d), lambda i, j: (0, 0)),
            pl.BlockSpec((d, tn), lambda i, j: (0, j)),
            pl.BlockSpec((1, tn), lambda i, j: (0, j)),
        ],
        out_specs=pl.BlockSpec((tm, tn), lambda i, j: (i, j)),
        out_shape=jax.ShapeDtypeStruct((s, n), out_dtype),
        scratch_shapes=[pltpu.VMEM((tm, d), BF16)],
        compiler_params=_cparams(("parallel", "arbitrary")),
        name="proj",
    )(x, g, w, gain)


def _log_sigmoid(z):
    return jnp.minimum(z, 0.0) - jnp.log1p(jnp.exp(-jnp.abs(z)))


def _cum_kernel(tail_ref, b_ref, o_ref, *, rows):
    s = tail_ref.shape[0]
    r_i = lax.broadcasted_iota(jnp.int32, (rows, rows), 0)
    c_i = lax.broadcasted_iota(jnp.int32, (rows, rows), 1)
    tri = (c_i <= r_i).astype(F32)

    def body(i, carry):
        start = pl.multiple_of(i * rows, rows)
        lf = _log_sigmoid(tail_ref[pl.ds(start, rows), :] + b_ref[...])
        c = jnp.dot(tri, lf, preferred_element_type=F32, precision=lax.Precision.HIGHEST) + carry
        o_ref[pl.ds(start, rows), :] = c * LOG2E
        return c[rows - 1:rows, :]

    lax.fori_loop(0, s // rows, body, jnp.zeros((1, LANES), F32))


def _cum(tail, b_pad, *, rows):
    s = tail.shape[0]
    return pl.pallas_call(
        functools.partial(_cum_kernel, rows=rows),
        grid=(1,),
        in_specs=[pl.BlockSpec((s, LANES), lambda i: (0, 0)), pl.BlockSpec((1, LANES), lambda i: (0, 0))],
        out_specs=pl.BlockSpec((s, LANES), lambda i: (0, 0)),
        out_shape=jax.ShapeDtypeStruct((s, LANES), F32),
        compiler_params=_cparams(("arbitrary",)),
        name="cum",
    )(tail, b_pad)


def _fox_prep_kernel(k_ref, cum_ref, o_ref):
    h = pl.program_id(1)
    lane = lax.broadcasted_iota(jnp.int32, cum_ref.shape, 1)
    neg = -jnp.sum(jnp.where(lane == h, cum_ref[...], 0.0), axis=-1, keepdims=True)
    hi = neg.astype(BF16).astype(F32)
    mid = (neg - hi).astype(BF16).astype(F32)
    lo = (neg - hi - mid).astype(BF16).astype(F32)
    aug = jnp.where(lane == 0, hi, jnp.where(lane == 1, mid, jnp.where(lane == 2, lo, 0.0)))
    o_ref[:, :LANES] = k_ref[...]
    o_ref[:, LANES:] = aug.astype(BF16)


def _fox_prep(main, cum, *, tm, k_off):
    s = main.shape[0]
    return pl.pallas_call(
        _fox_prep_kernel,
        grid=(s // tm, FOX_HEADS),
        in_specs=[pl.BlockSpec((tm, FOX_HEAD_DIM), lambda i, h: (i, k_off + h)),
                  pl.BlockSpec((tm, LANES), lambda i, h: (i, 0))],
        out_specs=pl.BlockSpec((tm, 2 * LANES), lambda i, h: (i, h)),
        out_shape=jax.ShapeDtypeStruct((s, FOX_HEADS * 2 * LANES), BF16),
        compiler_params=_cparams(("parallel", "arbitrary")),
        name="fox_prep",
    )(main, cum)


def _rope_tile(v, cos, sin_signed):
    return v * cos + pltpu.roll(v, LANES // 2, 1) * sin_signed


def _mla_prep_kernel(cq_ref, ckv_ref, tail_ref, ang_ref, gql_ref, gkvl_ref, wq_ref, wk_ref, wv_ref,
                     gq_ref, gk_ref, q_ref, k_ref, v_ref, cqn_ref, ckvn_ref, kpe_ref, cos_ref, sin_ref,
                     *, q_scale):
    h = pl.program_id(1)

    @pl.when(h == 0)
    def _():
        cq = cq_ref[...].astype(F32)
        ms = jnp.mean(cq * cq, axis=-1, keepdims=True)
        cqn_ref[...] = (cq * lax.rsqrt(ms + NORM_EPS) * gql_ref[...]).astype(BF16)
        ckv = ckv_ref[...].astype(F32)
        ms = jnp.mean(ckv * ckv, axis=-1, keepdims=True)
        ckvn_ref[...] = (ckv * lax.rsqrt(ms + NORM_EPS) * gkvl_ref[...]).astype(BF16)
        lane = lax.broadcasted_iota(jnp.int32, tail_ref.shape, 1)
        is_x1 = (lane >= 32) & (lane < 64)
        is_x2 = lane >= 96
        kpe_ref[...] = jnp.where(is_x1 | is_x2, tail_ref[...], 0.0)
        ang = ang_ref[...]
        cos_ref[...] = jnp.cos(ang)
        sn = jnp.sin(ang)
        sin_ref[...] = jnp.where(is_x1, -sn, sn)

    cos = cos_ref[...]
    sin_signed = sin_ref[...]

    q = jnp.dot(cqn_ref[...], wq_ref[...], preferred_element_type=F32)
    ms = jnp.sum(q * q, axis=-1, keepdims=True) * (1.0 / MLA_QK_DIM)
    qn = q * (lax.rsqrt(ms + NORM_EPS) * q_scale) * gq_ref[...]
    q_ref[:, :LANES] = qn[:, :LANES].astype(BF16)
    q_ref[:, LANES:] = _rope_tile(qn[:, LANES:], cos, sin_signed).astype(BF16)

    kn = jnp.dot(ckvn_ref[...], wk_ref[...], preferred_element_type=F32)
    kpe = kpe_ref[...]
    ss = jnp.sum(kn * kn, axis=-1, keepdims=True) + jnp.sum(kpe * kpe, axis=-1, keepdims=True)
    r = lax.rsqrt(ss * (1.0 / MLA_QK_DIM) + NORM_EPS)
    k_ref[:, :LANES] = (kn * r * gk_ref[:, :LANES]).astype(BF16)
    k_ref[:, LANES:] = _rope_tile(kpe * r * gk_ref[:, LANES:], cos, sin_signed).astype(BF16)

    v_ref[...] = jnp.dot(ckvn_ref[...], wv_ref[...], preferred_element_type=F32).astype(BF16)


def _mla_prep(main, tail, ang, gql, gkvl, wq, wk, wv, gq, gk, *, tm, cq_blk, ckv_blk):
    s = main.shape[0]
    heads = MLA_HEADS
    return pl.pallas_call(
        functools.partial(_mla_prep_kernel, q_scale=MLA_QK_DIM ** -0.5 * LOG2E),
        grid=(s // tm, heads),
        in_specs=[
            pl.BlockSpec((tm, MLA_Q_RANK), lambda i, h: (i, cq_blk)),
            pl.BlockSpec((tm, MLA_KV_RANK), lambda i, h: (i, ckv_blk)),
            pl.BlockSpec((tm, LANES), lambda i, h: (i, 0)),
            pl.BlockSpec((tm, LANES), lambda i, h: (i, 0)),
            pl.BlockSpec((1, MLA_Q_RANK), lambda i, h: (0, 0)),
            pl.BlockSpec((1, MLA_KV_RANK), lambda i, h: (0, 0)),
            pl.BlockSpec((MLA_Q_RANK, MLA_QK_PAD), lambda i, h: (0, h)),
            pl.BlockSpec((MLA_KV_RANK, MLA_NOPE), lambda i, h: (0, h)),
            pl.BlockSpec((MLA_KV_RANK, MLA_V), lambda i, h: (0, h)),
            pl.BlockSpec((1, MLA_QK_PAD), lambda i, h: (0, 0)),
            pl.BlockSpec((1, MLA_QK_PAD), lambda i, h: (0, 0)),
        ],
        out_specs=[
            pl.BlockSpec((tm, MLA_QK_PAD), lambda i, h: (i, h)),
            pl.BlockSpec((tm, MLA_QK_PAD), lambda i, h: (i, h)),
            pl.BlockSpec((tm, MLA_V), lambda i, h: (i, h)),
        ],
        out_shape=[
            jax.ShapeDtypeStruct((s, heads * MLA_QK_PAD), BF16),
            jax.ShapeDtypeStruct((s, heads * MLA_QK_PAD), BF16),
            jax.ShapeDtypeStruct((s, heads * MLA_V), BF16),
        ],
        scratch_shapes=[
            pltpu.VMEM((tm, MLA_Q_RANK), BF16),
            pltpu.VMEM((tm, MLA_KV_RANK), BF16),
            pltpu.VMEM((tm, LANES), F32),
            pltpu.VMEM((tm, LANES), F32),
            pltpu.VMEM((tm, LANES), F32),
        ],
        compiler_params=_cparams(("parallel", "arbitrary")),
        name="mla_prep",
    )(main, main, tail, ang, gql, gkvl, wq, wk, wv, gq, gk)


def _attn_kernel(q_ref, k_ref, v_ref, o_ref, vt_ref, acc_ref, s0_ref, s1_ref, s2_ref, s3_ref, *, tq, tk, sub,
                 extend_q):
    qi = pl.program_id(1)
    s_len, dv = v_ref.shape
    nsub = tq // sub
    tr = min(512, s_len)

    @pl.when(qi == 0)
    def _():
        def transpose_block(c, carry):
            st = pl.multiple_of(c * tr, tr)
            vt_ref[:dv, pl.ds(st, tr)] = v_ref[pl.ds(st, tr), :].astype(F32).T.astype(BF16)
            return carry

        lax.fori_loop(0, s_len // tr, transpose_block, 0)
        vt_ref[dv:, :] = jnp.ones((vt_ref.shape[0] - dv, s_len), BF16)

    q = q_ref[...]
    if extend_q:
        lane = lax.broadcasted_iota(jnp.int32, (tq, LANES), 1)
        q = jnp.concatenate([q, jnp.where(lane < 3, 1.0, 0.0).astype(BF16)], axis=1)
    qs = [q[c * sub:(c + 1) * sub, :] for c in range(nsub)]
    acc_ref[...] = jnp.zeros(acc_ref.shape, F32)

    def logits(c, kstart, buf):
        k = k_ref[pl.ds(kstart, tk), :]
        s_refs[buf][...] = lax.dot_general(k, qs[c], (((1,), (1,)), ((), ())), preferred_element_type=F32)

    def finish(c, m, kstart, buf, key_minus_query):
        st = s_refs[buf][...]
        if key_minus_query is not None:
            key = lax.broadcasted_iota(jnp.int32, (tk, sub), 0)
            qry = lax.broadcasted_iota(jnp.int32, (tk, sub), 1)
            st = jnp.where(key + key_minus_query <= qry, st, -jnp.inf)
        m_new = jnp.maximum(m, jnp.max(st, axis=0, keepdims=True))
        alpha = jnp.exp2(m - m_new)
        p = jnp.exp2(st - m_new).astype(BF16)
        pv = jnp.dot(vt_ref[:, pl.ds(kstart, tk)], p, preferred_element_type=F32)
        acc_ref[c] = alpha * acc_ref[c] + pv
        return m_new

    s_refs = (s0_ref, s1_ref, s2_ref, s3_ref)
    nbuf = len(s_refs)
    ahead = 2
    span = [(c, t) for t in range(tq // tk) for c in range(nsub)]
    assert tk % sub == 0 and tq % tk == 0 and len(span) % nbuf == 0 and len(span) >= ahead
    tail = []
    for c in range(nsub):
        for t in range(tq // tk):
            if t * tk <= (c + 1) * sub - 1:
                fully_visible = (t + 1) * tk - 1 <= c * sub
                tail.append((c, t, None if fully_visible else t * tk - c * sub))
    assert [x[:2] for x in tail[:ahead]] == span[:ahead]
    for i in range(ahead):
        logits(span[i][0], span[i][1] * tk, i)

    def sweep(j, ms):
        base = pl.multiple_of(j * tq, tq)
        ms = list(ms)
        for i, (c, t) in enumerate(span):
            nc, nt = span[(i + ahead) % len(span)]
            nbase = base if i + ahead < len(span) else base + tq
            logits(nc, pl.multiple_of(nbase + nt * tk, tk), (i + ahead) % nbuf)
            ms[c] = finish(c, ms[c], pl.multiple_of(base + t * tk, tk), i % nbuf, None)
        return tuple(ms)

    ms = list(lax.fori_loop(0, qi, sweep, tuple(jnp.full((1, sub), -jnp.inf, F32) for _ in range(nsub))))
    base = pl.multiple_of(qi * tq, tq)
    for i, (c, t, off) in enumerate(tail):
        if i + ahead < len(tail):
            logits(tail[i + ahead][0], pl.multiple_of(base + tail[i + ahead][1] * tk, tk), (i + ahead) % nbuf)
        ms[c] = finish(c, ms[c], pl.multiple_of(base + t * tk, tk), i % nbuf, off)
    for c in range(nsub):
        a = acc_ref[c]
        o_ref[c * sub:(c + 1) * sub, :] = (a[:dv, :] / a[dv:dv + 1, :]).T.astype(o_ref.dtype)


def _attention(q, k, v, *, heads, dq_in, dk, dv, q_off, v_off, tq, tk, sub, name):
    s = q.shape[0]
    ones_rows = 16
    return pl.pallas_call(
        functools.partial(_attn_kernel, tq=tq, tk=tk, sub=sub, extend_q=dq_in != dk),
        grid=(heads, s // tq),
        in_specs=[
            pl.BlockSpec((tq, dq_in), lambda h, i: (i, q_off + h)),
            pl.BlockSpec((s, dk), lambda h, i: (0, h)),
            pl.BlockSpec((s, dv), lambda h, i: (0, v_off + h)),
        ],
        out_specs=pl.BlockSpec((tq, dv), lambda h, i: (i, h)),
        out_shape=jax.ShapeDtypeStruct((s, heads * dv), BF16),
        scratch_shapes=[pltpu.VMEM((dv + ones_rows, s), BF16), pltpu.VMEM((tq // sub, dv + ones_rows, sub), F32),
                        *[pltpu.VMEM((tk, sub), F32) for _ in range(4)]],
        compiler_params=_cparams(("parallel", "arbitrary")),
        name=name,
    )(q, k, v)


def _merge_kernel(ya_ref, yb_ref, ga_ref, gb_ref, bg_ref, w_ref, x_ref, o_ref):
    half = ya_ref.shape[1]
    ga = ga_ref[...].astype(F32) + bg_ref[:, :half]
    gb = gb_ref[...].astype(F32) + bg_ref[:, half:]
    ma = (ya_ref[...].astype(F32) / (1.0 + jnp.exp(-ga))).astype(BF16)
    mb = (yb_ref[...].astype(F32) / (1.0 + jnp.exp(-gb))).astype(BF16)
    acc = jnp.dot(ma, w_ref[:half, :], preferred_element_type=F32)
    acc = acc + jnp.dot(mb, w_ref[half:, :], preferred_element_type=F32)
    o_ref[...] = x_ref[...] + acc


def _merge(ya, yb, main, b_gate, w_out, x, *, tm, ga_blk, gb_blk):
    s, d = x.shape
    half = ya.shape[1]
    return pl.pallas_call(
        _merge_kernel,
        grid=(s // tm,),
        in_specs=[
            pl.BlockSpec((tm, half), lambda i: (i, 0)),
            pl.BlockSpec((tm, half), lambda i: (i, 0)),
            pl.BlockSpec((tm, half), lambda i: (i, ga_blk)),
            pl.BlockSpec((tm, half), lambda i: (i, gb_blk)),
            pl.BlockSpec((1, 2 * half), lambda i: (0, 0)),
            pl.BlockSpec((2 * half, d), lambda i: (0, 0), pipeline_mode=pl.Buffered(1)),
            pl.BlockSpec((tm, d), lambda i: (i, 0)),
        ],
        out_specs=pl.BlockSpec((tm, d), lambda i: (i, 0)),
        out_shape=jax.ShapeDtypeStruct((s, d), F32),
        compiler_params=_cparams(("parallel",)),
        name="merge",
    )(ya, yb, main, main, b_gate, w_out, x)


def _pairs():
    return [(p, q) for p in range(PEER_SORT) for q in range(PEER_SORT) if (p + 1) * (q + 1) <= PEER_SORT]


def _route_kernel(x_ref, g_ref, wq_ref, keys_ref, h2_ref, s2_ref, e2_ref, thr_ref, e1n_ref, a_ref, b_ref, s1_ref):
    x = x_ref[...]
    ms = jnp.mean(x * x, axis=-1, keepdims=True)
    h2 = (x * lax.rsqrt(ms + NORM_EPS) * g_ref[...]).astype(BF16)
    h2_ref[...] = h2
    qp = jnp.dot(h2, wq_ref[...], preferred_element_type=F32)
    neg_inf = -jnp.inf

    def scores_t(h, c):
        blk = qp[:, (2 * h + c) * PEER_HALF:(2 * h + c + 1) * PEER_HALF]
        return lax.dot_general(keys_ref[2 * h + c], blk, (((1,), (1,)), ((), ())),
                               preferred_element_type=F32, precision=lax.Precision.HIGHEST)

    def top_values(st, dst_ref, h):
        work = st
        for r in range(PEER_SORT):
            m = jnp.max(work, axis=0, keepdims=True)
            dst_ref[r, h:h + 1, :] = m
            if r + 1 < PEER_SORT:
                work = jnp.where(work == m, neg_inf, work)

    for h in range(PEER_HEADS):
        s1 = scores_t(h, 0)
        s2 = scores_t(h, 1)
        s1_ref[h] = s1
        s2_ref[h] = s2
        top_values(s1, a_ref, h)
        top_values(s2, b_ref, h)

    cands = [a_ref[p] + b_ref[q] for (p, q) in _pairs()]
    work = list(cands)
    tau = None
    for r in range(PEER_SORT):
        m = functools.reduce(jnp.maximum, work)
        if r + 1 == PEER_TOPK:
            tau = m
        if r + 1 < PEER_SORT:
            work = [jnp.where(c == m, neg_inf, c) for c in work]
    tau_mid = 0.5 * (tau + m)
    top = a_ref[0] + b_ref[0]
    z = functools.reduce(lambda u, w: u + w, [jnp.where(c >= tau, jnp.exp(c - top), 0.0) for c in cands])
    inv_z = 1.0 / z
    a0 = a_ref[0]
    b0 = b_ref[0]
    for h in range(PEER_HEADS):
        s1 = s1_ref[h]
        thr_ref[:, h, :] = tau_mid[h:h + 1, :] - s1
        e1n_ref[:, h, :] = jnp.exp(s1 - a0[h:h + 1, :]) * inv_z[h:h + 1, :]
        e2_ref[h] = jnp.exp(s2_ref[h] - b0[h:h + 1, :])


def _route(x1, g, wq, keys, *, tm):
    s, d = x1.shape
    route_shape = jax.ShapeDtypeStruct((PEER_HEADS, PEER_N_KEYS, s), F32)
    route_spec = pl.BlockSpec((PEER_HEADS, PEER_N_KEYS, tm), lambda i: (0, 0, i))
    by_key_shape = jax.ShapeDtypeStruct((PEER_N_KEYS, PEER_HEADS, s), F32)
    by_key_spec = pl.BlockSpec((PEER_N_KEYS, PEER_HEADS, tm), lambda i: (0, 0, i))
    return pl.pallas_call(
        _route_kernel,
        grid=(s // tm,),
        in_specs=[
            pl.BlockSpec((tm, d), lambda i: (i, 0)),
            pl.BlockSpec((1, d), lambda i: (0, 0)),
            pl.BlockSpec(wq.shape, lambda i: (0, 0), pipeline_mode=pl.Buffered(1)),
            pl.BlockSpec(keys.shape, lambda i: (0, 0, 0)),
        ],
        out_specs=[pl.BlockSpec((tm, d), lambda i: (i, 0)), route_spec, route_spec, by_key_spec, by_key_spec],
        out_shape=[jax.ShapeDtypeStruct((s, d), BF16), route_shape, route_shape, by_key_shape, by_key_shape],
        scratch_shapes=[pltpu.VMEM((PEER_SORT, PEER_HEADS, tm), F32), pltpu.VMEM((PEER_SORT, PEER_HEADS, tm), F32),
                        pltpu.VMEM((PEER_HEADS, PEER_N_KEYS, tm), F32)],
        compiler_params=_cparams(("parallel",)),
        name="route",
    )(x1, g, wq, keys)


def _gelu(a):
    return 0.5 * a * (1.0 + lax.erf(a * (2.0 ** -0.5)))


def _peer_kernel(h2_ref, u_ref, v_ref, s2_ref, e2_ref, thr_ref, e1n_ref, x1_ref, o_ref, w_ref):
    ei = pl.program_id(1)
    te = u_ref.shape[0]
    tb = h2_ref.shape[0]
    ni = te // PEER_N_KEYS

    @pl.when(ei == 0)
    def _():
        o_ref[...] = x1_ref[...]

    a_t = lax.dot_general(u_ref[...], h2_ref[...], (((1,), (1,)), ((), ())), preferred_element_type=F32)
    for ii in range(ni):
        for tt in range(tb // LANES):
            tok = slice(tt * LANES, (tt + 1) * LANES)
            g = jnp.zeros((PEER_N_KEYS, LANES), F32)
            for h in range(PEER_HEADS):
                thr = thr_ref[ii, h:h + 1, tok]
                e1n = e1n_ref[ii, h:h + 1, tok]
                g = g + jnp.where(s2_ref[h, :, tok] >= thr, e2_ref[h, :, tok], 0.0) * e1n
            w = g * _gelu(a_t[ii * PEER_N_KEYS:(ii + 1) * PEER_N_KEYS, tok])
            w_ref[tok, ii * PEER_N_KEYS:(ii + 1) * PEER_N_KEYS] = w.T.astype(BF16)
    o_ref[...] += jnp.dot(w_ref[...], v_ref[...], preferred_element_type=F32)


def _peer(h2, u, v, s2, e2, thr, e1n, x1, *, tb, te):
    s, d = x1.shape
    ne = u.shape[0]
    route_spec = pl.BlockSpec((PEER_HEADS, PEER_N_KEYS, tb), lambda i, e: (0, 0, i))
    by_key_spec = pl.BlockSpec((te // PEER_N_KEYS, PEER_HEADS, tb), lambda i, e: (e, 0, i))
    return pl.pallas_call(
        _peer_kernel,
        grid=(s // tb, ne // te),
        in_specs=[
            pl.BlockSpec((tb, d), lambda i, e: (i, 0)),
            pl.BlockSpec((te, d), lambda i, e: (e, 0)),
            pl.BlockSpec((te, d), lambda i, e: (e, 0)),
            route_spec, route_spec, by_key_spec, by_key_spec,
            pl.BlockSpec((tb, d), lambda i, e: (i, 0)),
        ],
        out_specs=pl.BlockSpec((tb, d), lambda i, e: (i, 0)),
        out_shape=jax.ShapeDtypeStruct((s, d), F32),
        scratch_shapes=[pltpu.VMEM((tb, te), BF16)],
        compiler_params=_cparams(("parallel", "arbitrary")),
        name="peer",
    )(h2, u, v, s2, e2, thr, e1n, x1)


def _blocks(s):
    return dict(
        proj_tm=min(1024, s), proj_tn=1024,
        cum_rows=min(256, s),
        prep_tm=min(1024, s),
        attn_tq=min(1024, s), attn_tk=min(512, s), attn_sub=min(256, s),
        merge_tm=min(256, s),
        route_tm=min(512, s),
        peer_tb=min(512, s), peer_te=512,
    )


def kernel(x, positions, mix_norm_g, w_in, b_forget, b_gate, mla_q_latent_g, w_q_up, mla_kv_latent_g, w_kv_up,
           mla_q_norm_g, mla_k_norm_g, fox_q_norm_g, fox_k_norm_g, w_out, ffn_norm_g, w_peer_q, peer_sub_keys,
           peer_u, peer_v):
    bsz, s, d = x.shape
    assert bsz == 1 and mix_norm_g.shape[0] == 1
    blk = _blocks(s)
    half_r = MLA_ROPE // 2
    fox_w = FOX_HEADS * FOX_HEAD_DIM
    mla_w = MLA_HEADS * MLA_V

    wi = w_in[0]
    o = 0
    parts = {}
    for name, width in (("cq", MLA_Q_RANK), ("ckv", MLA_KV_RANK), ("kpe", MLA_ROPE), ("fq", fox_w), ("fk", fox_w),
                        ("fv", fox_w), ("fl", FOX_HEADS), ("ga", mla_w), ("gb", fox_w)):
        parts[name] = wi[:, o:o + width]
        o += width
    w_main = jnp.concatenate([parts[n] for n in ("fq", "fk", "fv", "ga", "gb", "cq", "ckv")], axis=1).astype(BF16)
    zcol = lambda n: jnp.zeros((d, n), F32)
    w_tail = jnp.concatenate([parts["fl"], zcol(16), parts["kpe"][:, :half_r], zcol(32), parts["kpe"][:, half_r:]],
                             axis=1).astype(BF16)
    n_main = w_main.shape[1]
    fox_scale = FOX_HEAD_DIM ** -0.5 * LOG2E
    gain_main = jnp.concatenate([jnp.tile(fox_q_norm_g[0], FOX_HEADS) * fox_scale, jnp.tile(fox_k_norm_g[0], FOX_HEADS),
                                 jnp.ones((n_main - 2 * fox_w,), F32)])[None, :]
    x2 = x[0]
    g_mix = mix_norm_g[0][None, :]

    tn = blk["proj_tn"]
    main = _proj(x2, g_mix, w_main, gain_main, tm=blk["proj_tm"], tn=tn, n_norm_tiles=2 * fox_w // tn, out_dtype=BF16)
    tail = _proj(x2, g_mix, w_tail, jnp.ones((1, LANES), F32), tm=blk["proj_tm"], tn=LANES, n_norm_tiles=0,
                 out_dtype=F32)

    b_pad = jnp.concatenate([b_forget[0], jnp.zeros((LANES - FOX_HEADS,), F32)])[None, :]
    cum = _cum(tail, b_pad, rows=blk["cum_rows"])
    k_b = _fox_prep(main, cum, tm=blk["prep_tm"], k_off=FOX_HEADS)

    def lane_layout(a, b):
        z = jnp.zeros_like(a)
        return jnp.concatenate([z, a, z, b], axis=-1)

    wq3 = w_q_up[0].reshape(MLA_Q_RANK, MLA_HEADS, MLA_QK_DIM)
    wq_pad = jnp.concatenate([wq3[..., :MLA_NOPE],
                              lane_layout(wq3[..., MLA_NOPE:MLA_NOPE + half_r], wq3[..., MLA_NOPE + half_r:])],
                             axis=-1).reshape(MLA_Q_RANK, MLA_HEADS * MLA_QK_PAD).astype(BF16)
    wkv3 = w_kv_up[0].reshape(MLA_KV_RANK, MLA_HEADS, MLA_NOPE + MLA_V)
    wk = wkv3[..., :MLA_NOPE].reshape(MLA_KV_RANK, MLA_HEADS * MLA_NOPE).astype(BF16)
    wv = wkv3[..., MLA_NOPE:].reshape(MLA_KV_RANK, MLA_HEADS * MLA_V).astype(BF16)

    def gain_layout(gv):
        return jnp.concatenate([gv[:MLA_NOPE], lane_layout(gv[MLA_NOPE:MLA_NOPE + half_r], gv[MLA_NOPE + half_r:])])[None, :]

    inv_freq = ROPE_THETA ** (-jnp.arange(half_r, dtype=F32) / half_r)
    ang = positions[0].astype(F32)[:, None] * lane_layout(inv_freq, inv_freq)[None, :]
    q_a, k_a, v_a = _mla_prep(main, tail, ang, mla_q_latent_g[0][None, :], mla_kv_latent_g[0][None, :], wq_pad, wk, wv,
                              gain_layout(mla_q_norm_g[0]), gain_layout(mla_k_norm_g[0]), tm=blk["prep_tm"],
                              cq_blk=(n_main - MLA_Q_RANK - MLA_KV_RANK) // MLA_Q_RANK,
                              ckv_blk=(n_main - MLA_KV_RANK) // MLA_KV_RANK)

    tiles = dict(tq=blk["attn_tq"], tk=blk["attn_tk"], sub=blk["attn_sub"])
    y_a = _attention(q_a, k_a, v_a, heads=MLA_HEADS, dq_in=MLA_QK_PAD, dk=MLA_QK_PAD, dv=MLA_V, q_off=0, v_off=0,
                     name="attn_mla", **tiles)
    y_b = _attention(main, k_b, main, heads=FOX_HEADS, dq_in=FOX_HEAD_DIM, dk=2 * LANES, dv=FOX_HEAD_DIM, q_off=0,
                     v_off=2 * FOX_HEADS, name="attn_fox", **tiles)

    x1 = _merge(y_a, y_b, main, b_gate[0][None, :], w_out[0].astype(BF16), x2, tm=blk["merge_tm"],
                ga_blk=3 * fox_w // mla_w, gb_blk=(3 * fox_w + mla_w) // fox_w)

    keys = peer_sub_keys[0].reshape(PEER_HEADS * 2, PEER_N_KEYS, PEER_HALF)
    h2, s2, e2, thr, e1n = _route(x1, ffn_norm_g[0][None, :], w_peer_q[0].astype(BF16), keys, tm=blk["route_tm"])
    out = _peer(h2, peer_u[0].astype(BF16), peer_v[0].astype(BF16), s2, e2, thr, e1n, x1, tb=blk["peer_tb"],
                te=blk["peer_te"])
    return out[None]
```

```python
import functools
import math

import jax
import jax.numpy as jnp
from jax import lax
from jax.experimental import pallas as pl
from jax.experimental.pallas import tpu as pltpu

F32 = jnp.float32
BF16 = jnp.bfloat16

LANES = 128
LOG2E = math.log2(math.e)
NORM_EPS = 1e-6
ROPE_THETA = 10000.0
MLA_HEADS = 16
MLA_Q_RANK = 512
MLA_KV_RANK = 512
MLA_NOPE = 128
MLA_ROPE = 64
MLA_V = 128
MLA_QK_DIM = MLA_NOPE + MLA_ROPE
MLA_QK_PAD = 2 * LANES
FOX_HEADS = 16
FOX_HEAD_DIM = 128
PEER_HEADS = 8
PEER_HALF = 128
PEER_N_KEYS = 128
PEER_TOPK = 16
PEER_SORT = PEER_TOPK + 1
VMEM_LIMIT = 56 * 1024 * 1024


def _cparams(sem, flags=None):
    return pltpu.CompilerParams(dimension_semantics=sem, vmem_limit_bytes=VMEM_LIMIT, flags=flags)


def _proj_kernel(x_ref, g_ref, w_ref, gain_ref, o_ref, hn_ref, *, n_norm_tiles):
    j = pl.program_id(1)

    @pl.when(j == 0)
    def _():
        x = x_ref[...]
        ms = jnp.mean(x * x, axis=-1, keepdims=True)
        hn_ref[...] = (x * lax.rsqrt(ms + NORM_EPS) * g_ref[...]).astype(BF16)

    acc = jnp.dot(hn_ref[...], w_ref[...], preferred_element_type=F32)
    tn = acc.shape[1]

    @pl.when(j < n_norm_tiles)
    def _():
        for c in range(tn // LANES):
            sl = slice(c * LANES, (c + 1) * LANES)
            a = acc[:, sl]
            ms = jnp.mean(a * a, axis=-1, keepdims=True)
            o_ref[:, sl] = (a * lax.rsqrt(ms + NORM_EPS) * gain_ref[:, sl]).astype(o_ref.dtype)

    @pl.when(j >= n_norm_tiles)
    def _():
        o_ref[...] = acc.astype(o_ref.dtype)


def _proj(x, g, w, gain, *, tm, tn, n_norm_tiles, out_dtype):
    s, d = x.shape
    n = w.shape[1]
    return pl.pallas_call(
        functools.partial(_proj_kernel, n_norm_tiles=n_norm_tiles),
        grid=(s // tm, n // tn),
        in_specs=[
            pl.BlockSpec((tm, d), lambda i, j: (i, 0)),
            pl.BlockSpec((1, d), lambda i, j: (0, 0)),
            pl.BlockSpec((d, tn), lambda i, j: (0, j)),
            pl.BlockSpec((1, tn), lambda i, j: (0, j)),
        ],
        out_specs=pl.BlockSpec((tm, tn), lambda i, j: (i, j)),
        out_shape=jax.ShapeDtypeStruct((s, n), out_dtype),
        scratch_shapes=[pltpu.VMEM((tm, d), BF16)],
        compiler_params=_cparams(("parallel", "arbitrary")),
        name="proj",
    )(x, g, w, gain)


def _log_sigmoid(z):
    return jnp.minimum(z, 0.0) - jnp.log1p(jnp.exp(-jnp.abs(z)))


def _cum_kernel(tail_ref, b_ref, o_ref, *, rows):
    s = tail_ref.shape[0]
    r_i = lax.broadcasted_iota(jnp.int32, (rows, rows), 0)
    c_i = lax.broadcasted_iota(jnp.int32, (rows, rows), 1)
    tri = (c_i <= r_i).astype(F32)

    def body(i, carry):
        start = pl.multiple_of(i * rows, rows)
        lf = _log_sigmoid(tail_ref[pl.ds(start, rows), :] + b_ref[...])
        c = jnp.dot(tri, lf, preferred_element_type=F32, precision=lax.Precision.HIGHEST) + carry
        o_ref[pl.ds(start, rows), :] = c * LOG2E
        return c[rows - 1:rows, :]

    lax.fori_loop(0, s // rows, body, jnp.zeros((1, LANES), F32))


def _cum(tail, b_pad, *, rows):
    s = tail.shape[0]
    return pl.pallas_call(
        functools.partial(_cum_kernel, rows=rows),
        grid=(1,),
        in_specs=[pl.BlockSpec((s, LANES), lambda i: (0, 0)), pl.BlockSpec((1, LANES), lambda i: (0, 0))],
        out_specs=pl.BlockSpec((s, LANES), lambda i: (0, 0)),
        out_shape=jax.ShapeDtypeStruct((s, LANES), F32),
        compiler_params=_cparams(("arbitrary",)),
        name="cum",
    )(tail, b_pad)


def _fox_prep_kernel(k_ref, cum_ref, o_ref):
    lane = lax.broadcasted_iota(jnp.int32, cum_ref.shape, 1)
    cum = cum_ref[...]
    for h in range(FOX_HEADS):
        neg = -jnp.sum(jnp.where(lane == h, cum, 0.0), axis=-1, keepdims=True)
        hi = neg.astype(BF16).astype(F32)
        mid = (neg - hi).astype(BF16).astype(F32)
        lo = (neg - hi - mid).astype(BF16).astype(F32)
        aug = jnp.where(lane == 0, hi, jnp.where(lane == 1, mid, jnp.where(lane == 2, lo, 0.0)))
        o_ref[:, 2 * h * LANES:(2 * h + 1) * LANES] = k_ref[:, h * LANES:(h + 1) * LANES]
        o_ref[:, (2 * h + 1) * LANES:(2 * h + 2) * LANES] = aug.astype(BF16)


def _fox_prep(main, cum, *, tm, k_blk):
    s = main.shape[0]
    width = FOX_HEADS * FOX_HEAD_DIM
    return pl.pallas_call(
        _fox_prep_kernel,
        grid=(s // tm,),
        in_specs=[pl.BlockSpec((tm, width), lambda i: (i, k_blk)),
                  pl.BlockSpec((tm, LANES), lambda i: (i, 0))],
        out_specs=pl.BlockSpec((tm, 2 * width), lambda i: (i, 0)),
        out_shape=jax.ShapeDtypeStruct((s, 2 * width), BF16),
        compiler_params=_cparams(("parallel",)),
        name="fox_prep",
    )(main, cum)


def _mla_prep_kernel(cq_ref, ckv_ref, tail_ref, ang_ref, gql_ref, gkvl_ref, wq_ref, wk_ref, wv_ref,
                     gq_ref, gk_ref, q_ref, k_ref, v_ref, cqn_ref, ckvn_ref, kpe_ref, cos_ref, sin_ref,
                     *, q_scale):
    h = pl.program_id(1)

    @pl.when(h == 0)
    def _():
        cq = cq_ref[...].astype(F32)
        ms = jnp.mean(cq * cq, axis=-1, keepdims=True)
        cqn_ref[...] = (cq * lax.rsqrt(ms + NORM_EPS) * gql_ref[...]).astype(BF16)
        ckv = ckv_ref[...].astype(F32)
        ms = jnp.mean(ckv * ckv, axis=-1, keepdims=True)
        ckvn_ref[...] = (ckv * lax.rsqrt(ms + NORM_EPS) * gkvl_ref[...]).astype(BF16)
        lane = lax.broadcasted_iota(jnp.int32, tail_ref.shape, 1)
        is_x1 = (lane >= 32) & (lane < 64)
        is_x2 = lane >= 96
        kpe_ref[...] = jnp.where(is_x1 | is_x2, tail_ref[...], 0.0)
        ang = ang_ref[...]
        cos_ref[...] = jnp.cos(ang)
        sn = jnp.sin(ang)
        sin_ref[...] = jnp.where(is_x1, -sn, sn)

    cos = cos_ref[...]
    sin_signed = sin_ref[...]
    ones = jnp.ones((LANES, LANES), BF16)
    m_i = lax.broadcasted_iota(jnp.int32, (LANES, LANES), 0)
    l_i = lax.broadcasted_iota(jnp.int32, (LANES, LANES), 1)
    swap = jnp.where(m_i == ((l_i + LANES // 2) & (LANES - 1)), 1.0, 0.0).astype(BF16)

    def row_sumsq(a):
        return jnp.dot((a * a).astype(BF16), ones, preferred_element_type=F32)

    def rope(v):
        return v * cos + jnp.dot(v.astype(BF16), swap, preferred_element_type=F32) * sin_signed

    q = jnp.dot(cqn_ref[...], wq_ref[...], preferred_element_type=F32)
    q_lo, q_hi = q[:, :LANES], q[:, LANES:]
    ms = (row_sumsq(q_lo) + row_sumsq(q_hi)) * (1.0 / MLA_QK_DIM)
    r = lax.rsqrt(ms + NORM_EPS) * q_scale
    q_ref[:, :LANES] = (q_lo * r * gq_ref[:, :LANES]).astype(BF16)
    q_ref[:, LANES:] = rope(q_hi * r * gq_ref[:, LANES:]).astype(BF16)

    kn = jnp.dot(ckvn_ref[...], wk_ref[...], preferred_element_type=F32)
    kpe = kpe_ref[...]
    r = lax.rsqrt((row_sumsq(kn) + row_sumsq(kpe)) * (1.0 / MLA_QK_DIM) + NORM_EPS)
    k_ref[:, :LANES] = (kn * r * gk_ref[:, :LANES]).astype(BF16)
    k_ref[:, LANES:] = rope(kpe * r * gk_ref[:, LANES:]).astype(BF16)

    v_ref[...] = jnp.dot(ckvn_ref[...], wv_ref[...], preferred_element_type=F32).astype(BF16)


def _mla_prep(main, tail, ang, gql, gkvl, wq, wk, wv, gq, gk, *, tm, cq_blk, ckv_blk):
    s = main.shape[0]
    heads = MLA_HEADS
    return pl.pallas_call(
        functools.partial(_mla_prep_kernel, q_scale=MLA_QK_DIM ** -0.5 * LOG2E),
        grid=(s // tm, heads),
        in_specs=[
            pl.BlockSpec((tm, MLA_Q_RANK), lambda i, h: (i, cq_blk)),
            pl.BlockSpec((tm, MLA_KV_RANK), lambda i, h: (i, ckv_blk)),
            pl.BlockSpec((tm, LANES), lambda i, h: (i, 0)),
            pl.BlockSpec((tm, LANES), lambda i, h: (i, 0)),
            pl.BlockSpec((1, MLA_Q_RANK), lambda i, h: (0, 0)),
            pl.BlockSpec((1, MLA_KV_RANK), lambda i, h: (0, 0)),
            pl.BlockSpec((MLA_Q_RANK, MLA_QK_PAD), lambda i, h: (0, h)),
            pl.BlockSpec((MLA_KV_RANK, MLA_NOPE), lambda i, h: (0, h)),
            pl.BlockSpec((MLA_KV_RANK, MLA_V), lambda i, h: (0, h)),
            pl.BlockSpec((1, MLA_QK_PAD), lambda i, h: (0, 0)),
            pl.BlockSpec((1, MLA_QK_PAD), lambda i, h: (0, 0)),
        ],
        out_specs=[
            pl.BlockSpec((tm, MLA_QK_PAD), lambda i, h: (i, h)),
            pl.BlockSpec((tm, MLA_QK_PAD), lambda i, h: (i, h)),
            pl.BlockSpec((tm, MLA_V), lambda i, h: (i, h)),
        ],
        out_shape=[
            jax.ShapeDtypeStruct((s, heads * MLA_QK_PAD), BF16),
            jax.ShapeDtypeStruct((s, heads * MLA_QK_PAD), BF16),
            jax.ShapeDtypeStruct((s, heads * MLA_V), BF16),
        ],
        scratch_shapes=[
            pltpu.VMEM((tm, MLA_Q_RANK), BF16),
            pltpu.VMEM((tm, MLA_KV_RANK), BF16),
            pltpu.VMEM((tm, LANES), F32),
            pltpu.VMEM((tm, LANES), F32),
            pltpu.VMEM((tm, LANES), F32),
        ],
        compiler_params=_cparams(("parallel", "arbitrary")),
        name="mla_prep",
    )(main, main, tail, ang, gql, gkvl, wq, wk, wv, gq, gk)


def _attn_kernel(q_ref, k_ref, v_ref, o_ref, vt_ref, acc_ref, s0_ref, s1_ref, s2_ref, s3_ref, *, tq, tk, sub,
                 extend_q):
    qi = pl.program_id(1)
    s_len, dv = v_ref.shape
    nsub = tq // sub
    tr = min(512, s_len)

    @pl.when(qi == 0)
    def _():
        def transpose_block(c, carry):
            st = pl.multiple_of(c * tr, tr)
            vt_ref[:dv, pl.ds(st, tr)] = v_ref[pl.ds(st, tr), :].astype(F32).T.astype(BF16)
            return carry

        lax.fori_loop(0, s_len // tr, transpose_block, 0)
        vt_ref[dv:, :] = jnp.ones((vt_ref.shape[0] - dv, s_len), BF16)

    q = q_ref[...]
    if extend_q:
        lane = lax.broadcasted_iota(jnp.int32, (tq, LANES), 1)
        q = jnp.concatenate([q, jnp.where(lane < 3, 1.0, 0.0).astype(BF16)], axis=1)
    qs = [q[c * sub:(c + 1) * sub, :] for c in range(nsub)]
    acc_ref[...] = jnp.zeros(acc_ref.shape, F32)

    def logits(c, kstart, buf):
        k = k_ref[pl.ds(kstart, tk), :]
        s_refs[buf][...] = lax.dot_general(k, qs[c], (((1,), (1,)), ((), ())), preferred_element_type=F32)

    def finish(c, m, kstart, buf, key_minus_query):
        st = s_refs[buf][...]
        if key_minus_query is not None:
            key = lax.broadcasted_iota(jnp.int32, (tk, sub), 0)
            qry = lax.broadcasted_iota(jnp.int32, (tk, sub), 1)
            st = jnp.where(key + key_minus_query <= qry, st, -jnp.inf)
        m_new = jnp.maximum(m, jnp.max(st, axis=0, keepdims=True))
        alpha = jnp.exp2(m - m_new)
        p = jnp.exp2(st - m_new).astype(BF16)
        pv = jnp.dot(vt_ref[:, pl.ds(kstart, tk)], p, preferred_element_type=F32)
        acc_ref[c] = alpha * acc_ref[c] + pv
        return m_new

    s_refs = (s0_ref, s1_ref, s2_ref, s3_ref)
    nbuf = len(s_refs)
    ahead = 2
    span = [(c, t) for t in range(tq // tk) for c in range(nsub)]
    assert tk % sub == 0 and tq % tk == 0 and len(span) % nbuf == 0 and len(span) >= ahead
    tail = []
    for c in range(nsub):
        for t in range(tq // tk):
            if t * tk <= (c + 1) * sub - 1:
                fully_visible = (t + 1) * tk - 1 <= c * sub
                tail.append((c, t, None if fully_visible else t * tk - c * sub))
    assert [x[:2] for x in tail[:ahead]] == span[:ahead]
    for i in range(ahead):
        logits(span[i][0], span[i][1] * tk, i)

    def sweep(j, ms):
        base = pl.multiple_of(j * tq, tq)
        ms = list(ms)
        for i, (c, t) in enumerate(span):
            nc, nt = span[(i + ahead) % len(span)]
            nbase = base if i + ahead < len(span) else base + tq
            logits(nc, pl.multiple_of(nbase + nt * tk, tk), (i + ahead) % nbuf)
            ms[c] = finish(c, ms[c], pl.multiple_of(base + t * tk, tk), i % nbuf, None)
        return tuple(ms)

    ms = list(lax.fori_loop(0, qi, sweep, tuple(jnp.full((1, sub), -jnp.inf, F32) for _ in range(nsub))))
    base = pl.multiple_of(qi * tq, tq)
    for i, (c, t, off) in enumerate(tail):
        if i + ahead < len(tail):
            logits(tail[i + ahead][0], pl.multiple_of(base + tail[i + ahead][1] * tk, tk), (i + ahead) % nbuf)
        ms[c] = finish(c, ms[c], pl.multiple_of(base + t * tk, tk), i % nbuf, off)
    for c in range(nsub):
        a = acc_ref[c]
        o_ref[c * sub:(c + 1) * sub, :] = (a[:dv, :] / a[dv:dv + 1, :]).T.astype(o_ref.dtype)


def _attention(q, k, v, *, heads, dq_in, dk, dv, q_off, v_off, tq, tk, sub, name):
    s = q.shape[0]
    ones_rows = 16
    return pl.pallas_call(
        functools.partial(_attn_kernel, tq=tq, tk=tk, sub=sub, extend_q=dq_in != dk),
        grid=(heads, s // tq),
        in_specs=[
            pl.BlockSpec((tq, dq_in), lambda h, i: (i, q_off + h)),
            pl.BlockSpec((s, dk), lambda h, i: (0, h)),
            pl.BlockSpec((s, dv), lambda h, i: (0, v_off + h)),
        ],
        out_specs=pl.BlockSpec((tq, dv), lambda h, i: (i, h)),
        out_shape=jax.ShapeDtypeStruct((s, heads * dv), BF16),
        scratch_shapes=[pltpu.VMEM((dv + ones_rows, s), BF16), pltpu.VMEM((tq // sub, dv + ones_rows, sub), F32),
                        *[pltpu.VMEM((tk, sub), F32) for _ in range(4)]],
        compiler_params=_cparams(("parallel", "arbitrary")),
        name=name,
    )(q, k, v)


def _merge_kernel(ya_ref, yb_ref, ga_ref, gb_ref, bg_ref, w_ref, x_ref, o_ref):
    half = ya_ref.shape[1]
    ga = ga_ref[...].astype(F32) + bg_ref[:, :half]
    gb = gb_ref[...].astype(F32) + bg_ref[:, half:]
    ma = (ya_ref[...].astype(F32) / (1.0 + jnp.exp(-ga))).astype(BF16)
    mb = (yb_ref[...].astype(F32) / (1.0 + jnp.exp(-gb))).astype(BF16)
    acc = jnp.dot(ma, w_ref[:half, :], preferred_element_type=F32)
    acc = acc + jnp.dot(mb, w_ref[half:, :], preferred_element_type=F32)
    o_ref[...] = x_ref[...] + acc


def _merge(ya, yb, main, b_gate, w_out, x, *, tm, ga_blk, gb_blk):
    s, d = x.shape
    half = ya.shape[1]
    return pl.pallas_call(
        _merge_kernel,
        grid=(s // tm,),
        in_specs=[
            pl.BlockSpec((tm, half), lambda i: (i, 0)),
            pl.BlockSpec((tm, half), lambda i: (i, 0)),
            pl.BlockSpec((tm, half), lambda i: (i, ga_blk)),
            pl.BlockSpec((tm, half), lambda i: (i, gb_blk)),
            pl.BlockSpec((1, 2 * half), lambda i: (0, 0)),
            pl.BlockSpec((2 * half, d), lambda i: (0, 0), pipeline_mode=pl.Buffered(1)),
            pl.BlockSpec((tm, d), lambda i: (i, 0)),
        ],
        out_specs=pl.BlockSpec((tm, d), lambda i: (i, 0)),
        out_shape=jax.ShapeDtypeStruct((s, d), F32),
        compiler_params=_cparams(("parallel",)),
        name="merge",
    )(ya, yb, main, main, b_gate, w_out, x)


def _pairs():
    return [(p, q) for p in range(PEER_SORT) for q in range(PEER_SORT) if (p + 1) * (q + 1) <= PEER_SORT]


def _route_kernel(x_ref, g_ref, wq_ref, keys_ref, h2_ref, s2_ref, e2_ref, thr_ref, e1n_ref, a_ref, b_ref, s1_ref):
    x = x_ref[...]
    ms = jnp.mean(x * x, axis=-1, keepdims=True)
    h2 = (x * lax.rsqrt(ms + NORM_EPS) * g_ref[...]).astype(BF16)
    h2_ref[...] = h2.astype(F32).T.astype(BF16)
    qp = jnp.dot(h2, wq_ref[...], preferred_element_type=F32)
    neg_inf = -jnp.inf

    def scores_t(h, c):
        blk = qp[:, (2 * h + c) * PEER_HALF:(2 * h + c + 1) * PEER_HALF]
        return lax.dot_general(keys_ref[2 * h + c], blk, (((1,), (1,)), ((), ())),
                               preferred_element_type=F32, precision=lax.Precision.HIGHEST)

    def top_values(st, dst_ref, h):
        work = st
        for r in range(PEER_SORT):
            m = jnp.max(work, axis=0, keepdims=True)
            dst_ref[r, h:h + 1, :] = m
            if r + 1 < PEER_SORT:
                work = jnp.where(work == m, neg_inf, work)

    for h in range(PEER_HEADS):
        s1 = scores_t(h, 0)
        s2 = scores_t(h, 1)
        s1_ref[h] = s1
        s2_ref[h] = s2
        top_values(s1, a_ref, h)
        top_values(s2, b_ref, h)

    cands = [a_ref[p] + b_ref[q] for (p, q) in _pairs()]
    work = list(cands)
    tau = None
    for r in range(PEER_SORT):
        m = functools.reduce(jnp.maximum, work)
        if r + 1 == PEER_TOPK:
            tau = m
        if r + 1 < PEER_SORT:
            work = [jnp.where(c == m, neg_inf, c) for c in work]
    tau_mid = 0.5 * (tau + m)
    top = a_ref[0] + b_ref[0]
    z = functools.reduce(lambda u, w: u + w, [jnp.where(c >= tau, jnp.exp(c - top), 0.0) for c in cands])
    inv_z = 1.0 / z
    a0 = a_ref[0]
    b0 = b_ref[0]
    for h in range(PEER_HEADS):
        s1 = s1_ref[h]
        thr_ref[:, h, :] = tau_mid[h:h + 1, :] - s1
        e1n_ref[:, h, :] = jnp.exp(s1 - a0[h:h + 1, :]) * inv_z[h:h + 1, :]
        e2_ref[h] = jnp.exp(s2_ref[h] - b0[h:h + 1, :])


def _route(x1, g, wq, keys, *, tm):
    s, d = x1.shape
    route_shape = jax.ShapeDtypeStruct((PEER_HEADS, PEER_N_KEYS, s), F32)
    route_spec = pl.BlockSpec((PEER_HEADS, PEER_N_KEYS, tm), lambda i: (0, 0, i))
    by_key_shape = jax.ShapeDtypeStruct((PEER_N_KEYS, PEER_HEADS, s), F32)
    by_key_spec = pl.BlockSpec((PEER_N_KEYS, PEER_HEADS, tm), lambda i: (0, 0, i))
    return pl.pallas_call(
        _route_kernel,
        grid=(s // tm,),
        in_specs=[
            pl.BlockSpec((tm, d), lambda i: (i, 0)),
            pl.BlockSpec((1, d), lambda i: (0, 0)),
            pl.BlockSpec(wq.shape, lambda i: (0, 0), pipeline_mode=pl.Buffered(1)),
            pl.BlockSpec(keys.shape, lambda i: (0, 0, 0)),
        ],
        out_specs=[pl.BlockSpec((d, tm), lambda i: (0, i)), route_spec, route_spec, by_key_spec, by_key_spec],
        out_shape=[jax.ShapeDtypeStruct((d, s), BF16), route_shape, route_shape, by_key_shape, by_key_shape],
        scratch_shapes=[pltpu.VMEM((PEER_SORT, PEER_HEADS, tm), F32), pltpu.VMEM((PEER_SORT, PEER_HEADS, tm), F32),
                        pltpu.VMEM((PEER_HEADS, PEER_N_KEYS, tm), F32)],
        compiler_params=_cparams(("parallel",)),
        name="route",
    )(x1, g, wq, keys)


def _gelu(a):
    return 0.5 * a * (1.0 + lax.erf(a * (2.0 ** -0.5)))


def _peer_kernel(h2_ref, u_ref, v_ref, s2_ref, e2_ref, thr_ref, e1n_ref, x1_ref, o_ref, w0_ref, w1_ref, a0_ref,
                 a1_ref, a2_ref, *, sub):
    ei = pl.program_id(1)
    te = u_ref.shape[0]
    tb = h2_ref.shape[1]
    ni = te // PEER_N_KEYS
    nsub = tb // sub
    a_refs = (a0_ref, a1_ref, a2_ref)[:min(3, nsub)]
    ahead = len(a_refs) - 1

    @pl.when(ei == 0)
    def _():
        o_ref[...] = x1_ref[...]

    def activations(n):
        a_refs[n % len(a_refs)][...] = jnp.dot(u_ref[...], h2_ref[:, n * sub:(n + 1) * sub],
                                               preferred_element_type=F32)

    w_refs = (w0_ref, w1_ref)

    def weights(n, ii):
        a_ref = a_refs[n % len(a_refs)]
        exp_rows = slice(ii * PEER_N_KEYS, (ii + 1) * PEER_N_KEYS)
        for tt in range(sub // LANES):
            tok = slice(n * sub + tt * LANES, n * sub + (tt + 1) * LANES)
            g = jnp.zeros((PEER_N_KEYS, LANES), F32)
            for h in range(PEER_HEADS):
                thr = thr_ref[ii, h:h + 1, tok]
                e1n = e1n_ref[ii, h:h + 1, tok]
                g = g + jnp.where(s2_ref[h, :, tok] >= thr, e2_ref[h, :, tok], 0.0) * e1n
            w = g * _gelu(a_ref[exp_rows, tt * LANES:(tt + 1) * LANES])
            w_refs[n % 2][tt * LANES:(tt + 1) * LANES, exp_rows] = w.T.astype(BF16)

    def mix(n):
        rows = slice(n * sub, (n + 1) * sub)
        o_ref[rows, :] += jnp.dot(w_refs[n % 2][...], v_ref[...], preferred_element_type=F32)

    for n in range(min(ahead, nsub)):
        activations(n)
    for n in range(nsub):
        for ii in range(ni):
            if ii == 1 and n + ahead < nsub:
                activations(n + ahead)
            if ii == ni - 1 and n > 0:
                mix(n - 1)
            weights(n, ii)
    mix(nsub - 1)


def _peer(h2, u, v, s2, e2, thr, e1n, x1, *, tb, te, sub):
    s, d = x1.shape
    ne = u.shape[0]
    route_spec = pl.BlockSpec((PEER_HEADS, PEER_N_KEYS, tb), lambda i, e: (0, 0, i))
    by_key_spec = pl.BlockSpec((te // PEER_N_KEYS, PEER_HEADS, tb), lambda i, e: (e, 0, i))
    return pl.pallas_call(
        functools.partial(_peer_kernel, sub=sub),
        grid=(s // tb, ne // te),
        in_specs=[
            pl.BlockSpec((d, tb), lambda i, e: (0, i)),
            pl.BlockSpec((te, d), lambda i, e: (e, 0)),
            pl.BlockSpec((te, d), lambda i, e: (e, 0)),
            route_spec, route_spec, by_key_spec, by_key_spec,
            pl.BlockSpec((tb, d), lambda i, e: (i, 0)),
        ],
        out_specs=pl.BlockSpec((tb, d), lambda i, e: (i, 0)),
        out_shape=jax.ShapeDtypeStruct((s, d), F32),
        scratch_shapes=[pltpu.VMEM((sub, te), BF16) for _ in range(2)] + [pltpu.VMEM((te, sub), F32) for _ in range(3)],
        compiler_params=_cparams(("parallel", "arbitrary")),
        name="peer",
    )(h2, u, v, s2, e2, thr, e1n, x1)


def _blocks(s):
    return dict(
        proj_tm=min(1024, s), proj_tn=1024,
        cum_rows=min(256, s),
        prep_tm=min(1024, s), fox_prep_tm=min(512, s),
        attn_tq=min(1024, s), attn_tk=min(512, s), attn_sub=min(256, s),
        merge_tm=min(256, s),
        route_tm=min(512, s),
        peer_tb=min(512, s), peer_te=512, peer_sub=min(256, s),
    )


def kernel(x, positions, mix_norm_g, w_in, b_forget, b_gate, mla_q_latent_g, w_q_up, mla_kv_latent_g, w_kv_up,
           mla_q_norm_g, mla_k_norm_g, fox_q_norm_g, fox_k_norm_g, w_out, ffn_norm_g, w_peer_q, peer_sub_keys,
           peer_u, peer_v):
    bsz, s, d = x.shape
    assert bsz == 1 and mix_norm_g.shape[0] == 1
    blk = _blocks(s)
    half_r = MLA_ROPE // 2
    fox_w = FOX_HEADS * FOX_HEAD_DIM
    mla_w = MLA_HEADS * MLA_V

    wi = w_in[0]
    o = 0
    parts = {}
    for name, width in (("cq", MLA_Q_RANK), ("ckv", MLA_KV_RANK), ("kpe", MLA_ROPE), ("fq", fox_w), ("fk", fox_w),
                        ("fv", fox_w), ("fl", FOX_HEADS), ("ga", mla_w), ("gb", fox_w)):
        parts[name] = wi[:, o:o + width]
        o += width
    w_main = jnp.concatenate([parts[n] for n in ("fq", "fk", "fv", "ga", "gb", "cq", "ckv")], axis=1).astype(BF16)
    zcol = lambda n: jnp.zeros((d, n), F32)
    w_tail = jnp.concatenate([parts["fl"], zcol(16), parts["kpe"][:, :half_r], zcol(32), parts["kpe"][:, half_r:]],
                             axis=1).astype(BF16)
    n_main = w_main.shape[1]
    fox_scale = FOX_HEAD_DIM ** -0.5 * LOG2E
    gain_main = jnp.concatenate([jnp.tile(fox_q_norm_g[0], FOX_HEADS) * fox_scale, jnp.tile(fox_k_norm_g[0], FOX_HEADS),
                                 jnp.ones((n_main - 2 * fox_w,), F32)])[None, :]
    x2 = x[0]
    g_mix = mix_norm_g[0][None, :]

    tn = blk["proj_tn"]
    main = _proj(x2, g_mix, w_main, gain_main, tm=blk["proj_tm"], tn=tn, n_norm_tiles=2 * fox_w // tn, out_dtype=BF16)
    tail = _proj(x2, g_mix, w_tail, jnp.ones((1, LANES), F32), tm=blk["proj_tm"], tn=LANES, n_norm_tiles=0,
                 out_dtype=F32)

    b_pad = jnp.concatenate([b_forget[0], jnp.zeros((LANES - FOX_HEADS,), F32)])[None, :]
    cum = _cum(tail, b_pad, rows=blk["cum_rows"])
    k_b = _fox_prep(main, cum, tm=blk["fox_prep_tm"], k_blk=1)

    def lane_layout(a, b):
        z = jnp.zeros_like(a)
        return jnp.concatenate([z, a, z, b], axis=-1)

    wq3 = w_q_up[0].reshape(MLA_Q_RANK, MLA_HEADS, MLA_QK_DIM)
    wq_pad = jnp.concatenate([wq3[..., :MLA_NOPE],
                              lane_layout(wq3[..., MLA_NOPE:MLA_NOPE + half_r], wq3[..., MLA_NOPE + half_r:])],
                             axis=-1).reshape(MLA_Q_RANK, MLA_HEADS * MLA_QK_PAD).astype(BF16)
    wkv3 = w_kv_up[0].reshape(MLA_KV_RANK, MLA_HEADS, MLA_NOPE + MLA_V)
    wk = wkv3[..., :MLA_NOPE].reshape(MLA_KV_RANK, MLA_HEADS * MLA_NOPE).astype(BF16)
    wv = wkv3[..., MLA_NOPE:].reshape(MLA_KV_RANK, MLA_HEADS * MLA_V).astype(BF16)

    def gain_layout(gv):
        return jnp.concatenate([gv[:MLA_NOPE], lane_layout(gv[MLA_NOPE:MLA_NOPE + half_r], gv[MLA_NOPE + half_r:])])[None, :]

    inv_freq = ROPE_THETA ** (-jnp.arange(half_r, dtype=F32) / half_r)
    ang = positions[0].astype(F32)[:, None] * lane_layout(inv_freq, inv_freq)[None, :]
    q_a, k_a, v_a = _mla_prep(main, tail, ang, mla_q_latent_g[0][None, :], mla_kv_latent_g[0][None, :], wq_pad, wk, wv,
                              gain_layout(mla_q_norm_g[0]), gain_layout(mla_k_norm_g[0]), tm=blk["prep_tm"],
                              cq_blk=(n_main - MLA_Q_RANK - MLA_KV_RANK) // MLA_Q_RANK,
                              ckv_blk=(n_main - MLA_KV_RANK) // MLA_KV_RANK)

    tiles = dict(tq=blk["attn_tq"], tk=blk["attn_tk"], sub=blk["attn_sub"])
    y_a = _attention(q_a, k_a, v_a, heads=MLA_HEADS, dq_in=MLA_QK_PAD, dk=MLA_QK_PAD, dv=MLA_V, q_off=0, v_off=0,
                     name="attn_mla", **tiles)
    y_b = _attention(main, k_b, main, heads=FOX_HEADS, dq_in=FOX_HEAD_DIM, dk=2 * LANES, dv=FOX_HEAD_DIM, q_off=0,
                     v_off=2 * FOX_HEADS, name="attn_fox", **tiles)

    x1 = _merge(y_a, y_b, main, b_gate[0][None, :], w_out[0].astype(BF16), x2, tm=blk["merge_tm"],
                ga_blk=3 * fox_w // mla_w, gb_blk=(3 * fox_w + mla_w) // fox_w)

    keys = peer_sub_keys[0].reshape(PEER_HEADS * 2, PEER_N_KEYS, PEER_HALF)
    h2, s2, e2, thr, e1n = _route(x1, ffn_norm_g[0][None, :], w_peer_q[0].astype(BF16), keys, tm=blk["route_tm"])
    out = _peer(h2, peer_u[0].astype(BF16), peer_v[0].astype(BF16), s2, e2, thr, e1n, x1, tb=blk["peer_tb"],
                te=blk["peer_te"], sub=blk["peer_sub"])
    return out[None]
```

```python
import functools
import math

import jax
import jax.numpy as jnp
from jax import lax
from jax.experimental import pallas as pl
from jax.experimental.pallas import tpu as pltpu

F32 = jnp.float32
BF16 = jnp.bfloat16

LANES = 128
LOG2E = math.log2(math.e)
NORM_EPS = 1e-6
ROPE_THETA = 10000.0
MLA_HEADS = 16
MLA_Q_RANK = 512
MLA_KV_RANK = 512
MLA_NOPE = 128
MLA_ROPE = 64
MLA_V = 128
MLA_QK_DIM = MLA_NOPE + MLA_ROPE
MLA_QK_PAD = 2 * LANES
FOX_HEADS = 16
FOX_HEAD_DIM = 128
PEER_HEADS = 8
PEER_HALF = 128
PEER_N_KEYS = 128
PEER_TOPK = 16
PEER_SORT = PEER_TOPK + 1
VMEM_LIMIT = 56 * 1024 * 1024


def _cparams(sem, flags=None):
    return pltpu.CompilerParams(dimension_semantics=sem, vmem_limit_bytes=VMEM_LIMIT, flags=flags)


def _proj_kernel(x_ref, g_ref, w_ref, gain_ref, o_ref, hn_ref, *, n_norm_tiles):
    j = pl.program_id(1)

    @pl.when(j == 0)
    def _():
        x = x_ref[...]
        ms = jnp.mean(x * x, axis=-1, keepdims=True)
        hn_ref[...] = (x * lax.rsqrt(ms + NORM_EPS) * g_ref[...]).astype(BF16)

    acc = jnp.dot(hn_ref[...], w_ref[...], preferred_element_type=F32)
    tn = acc.shape[1]

    @pl.when(j < n_norm_tiles)
    def _():
        for c in range(tn // LANES):
            sl = slice(c * LANES, (c + 1) * LANES)
            a = acc[:, sl]
            ms = jnp.mean(a * a, axis=-1, keepdims=True)
            o_ref[:, sl] = (a * lax.rsqrt(ms + NORM_EPS) * gain_ref[:, sl]).astype(o_ref.dtype)

    @pl.when(j >= n_norm_tiles)
    def _():
        o_ref[...] = acc.astype(o_ref.dtype)


def _proj(x, g, w, gain, *, tm, tn, n_norm_tiles, out_dtype):
    s, d = x.shape
    n = w.shape[1]
    return pl.pallas_call(
        functools.partial(_proj_kernel, n_norm_tiles=n_norm_tiles),
        grid=(s // tm, n // tn),
        in_specs=[
            pl.BlockSpec((tm, d), lambda i, j: (i, 0)),
            pl.BlockSpec((1, d), lambda i, j: (0, 0)),
            pl.BlockSpec((d, tn), lambda i, j: (0, j)),
            pl.BlockSpec((1, tn), lambda i, j: (0, j)),
        ],
        out_specs=pl.BlockSpec((tm, tn), lambda i, j: (i, j)),
        out_shape=jax.ShapeDtypeStruct((s, n), out_dtype),
        scratch_shapes=[pltpu.VMEM((tm, d), BF16)],
        compiler_params=_cparams(("parallel", "arbitrary")),
        name="proj",
    )(x, g, w, gain)


def _log_sigmoid(z):
    return jnp.minimum(z, 0.0) - jnp.log1p(jnp.exp(-jnp.abs(z)))


def _cum_kernel(tail_ref, b_ref, o_ref, *, rows):
    s = tail_ref.shape[0]
    r_i = lax.broadcasted_iota(jnp.int32, (rows, rows), 0)
    c_i = lax.broadcasted_iota(jnp.int32, (rows, rows), 1)
    tri = (c_i <= r_i).astype(F32)

    def body(i, carry):
        start = pl.multiple_of(i * rows, rows)
        lf = _log_sigmoid(tail_ref[pl.ds(start, rows), :] + b_ref[...])
        c = jnp.dot(tri, lf, preferred_element_type=F32, precision=lax.Precision.HIGHEST) + carry
        o_ref[pl.ds(start, rows), :] = c * LOG2E
        return c[rows - 1:rows, :]

    lax.fori_loop(0, s // rows, body, jnp.zeros((1, LANES), F32))


def _cum(tail, b_pad, *, rows):
    s = tail.shape[0]
    return pl.pallas_call(
        functools.partial(_cum_kernel, rows=rows),
        grid=(1,),
        in_specs=[pl.BlockSpec((s, LANES), lambda i: (0, 0)), pl.BlockSpec((1, LANES), lambda i: (0, 0))],
        out_specs=pl.BlockSpec((s, LANES), lambda i: (0, 0)),
        out_shape=jax.ShapeDtypeStruct((s, LANES), F32),
        compiler_params=_cparams(("arbitrary",)),
        name="cum",
    )(tail, b_pad)


def _fox_prep_kernel(k_ref, cum_ref, o_ref):
    lane = lax.broadcasted_iota(jnp.int32, cum_ref.shape, 1)
    cum = cum_ref[...]
    for h in range(FOX_HEADS):
        neg = -jnp.sum(jnp.where(lane == h, cum, 0.0), axis=-1, keepdims=True)
        hi = neg.astype(BF16).astype(F32)
        mid = (neg - hi).astype(BF16).astype(F32)
        lo = (neg - hi - mid).astype(BF16).astype(F32)
        aug = jnp.where(lane == 0, hi, jnp.where(lane == 1, mid, jnp.where(lane == 2, lo, 0.0)))
        o_ref[:, 2 * h * LANES:(2 * h + 1) * LANES] = k_ref[:, h * LANES:(h + 1) * LANES]
        o_ref[:, (2 * h + 1) * LANES:(2 * h + 2) * LANES] = aug.astype(BF16)


def _fox_prep(main, cum, *, tm, k_blk):
    s = main.shape[0]
    width = FOX_HEADS * FOX_HEAD_DIM
    return pl.pallas_call(
        _fox_prep_kernel,
        grid=(s // tm,),
        in_specs=[pl.BlockSpec((tm, width), lambda i: (i, k_blk)),
                  pl.BlockSpec((tm, LANES), lambda i: (i, 0))],
        out_specs=pl.BlockSpec((tm, 2 * width), lambda i: (i, 0)),
        out_shape=jax.ShapeDtypeStruct((s, 2 * width), BF16),
        compiler_params=_cparams(("parallel",)),
        name="fox_prep",
    )(main, cum)


def _mla_prep_kernel(cq_ref, ckv_ref, tail_ref, ang_ref, gql_ref, gkvl_ref, wq_ref, wk_ref, wv_ref,
                     gq_ref, gk_ref, q_ref, k_ref, v_ref, cqn_ref, ckvn_ref, kpe_ref, cos_ref, sin_ref,
                     *, q_scale):
    h = pl.program_id(1)

    @pl.when(h == 0)
    def _():
        cq = cq_ref[...].astype(F32)
        ms = jnp.mean(cq * cq, axis=-1, keepdims=True)
        cqn_ref[...] = (cq * lax.rsqrt(ms + NORM_EPS) * gql_ref[...]).astype(BF16)
        ckv = ckv_ref[...].astype(F32)
        ms = jnp.mean(ckv * ckv, axis=-1, keepdims=True)
        ckvn_ref[...] = (ckv * lax.rsqrt(ms + NORM_EPS) * gkvl_ref[...]).astype(BF16)
        lane = lax.broadcasted_iota(jnp.int32, tail_ref.shape, 1)
        is_x1 = (lane >= 32) & (lane < 64)
        is_x2 = lane >= 96
        kpe_ref[...] = jnp.where(is_x1 | is_x2, tail_ref[...], 0.0)
        ang = ang_ref[...]
        cos_ref[...] = jnp.cos(ang)
        sn = jnp.sin(ang)
        sin_ref[...] = jnp.where(is_x1, -sn, sn)

    cos = cos_ref[...]
    sin_signed = sin_ref[...]
    ones = jnp.ones((LANES, LANES), BF16)
    m_i = lax.broadcasted_iota(jnp.int32, (LANES, LANES), 0)
    l_i = lax.broadcasted_iota(jnp.int32, (LANES, LANES), 1)
    swap = jnp.where(m_i == ((l_i + LANES // 2) & (LANES - 1)), 1.0, 0.0).astype(BF16)

    def row_sumsq(a):
        return jnp.dot((a * a).astype(BF16), ones, preferred_element_type=F32)

    def rope(v):
        return v * cos + jnp.dot(v.astype(BF16), swap, preferred_element_type=F32) * sin_signed

    q = jnp.dot(cqn_ref[...], wq_ref[...], preferred_element_type=F32)
    q_lo, q_hi = q[:, :LANES], q[:, LANES:]
    ms = (row_sumsq(q_lo) + row_sumsq(q_hi)) * (1.0 / MLA_QK_DIM)
    r = lax.rsqrt(ms + NORM_EPS) * q_scale
    q_ref[:, :LANES] = (q_lo * r * gq_ref[:, :LANES]).astype(BF16)
    q_ref[:, LANES:] = rope(q_hi * r * gq_ref[:, LANES:]).astype(BF16)

    kn = jnp.dot(ckvn_ref[...], wk_ref[...], preferred_element_type=F32)
    kpe = kpe_ref[...]
    r = lax.rsqrt((row_sumsq(kn) + row_sumsq(kpe)) * (1.0 / MLA_QK_DIM) + NORM_EPS)
    k_ref[:, :LANES] = (kn * r * gk_ref[:, :LANES]).astype(BF16)
    k_ref[:, LANES:] = rope(kpe * r * gk_ref[:, LANES:]).astype(BF16)

    v_ref[...] = jnp.dot(ckvn_ref[...], wv_ref[...], preferred_element_type=F32).astype(BF16)


def _mla_prep(main, tail, ang, gql, gkvl, wq, wk, wv, gq, gk, *, tm, cq_blk, ckv_blk):
    s = main.shape[0]
    heads = MLA_HEADS
    return pl.pallas_call(
        functools.partial(_mla_prep_kernel, q_scale=MLA_QK_DIM ** -0.5 * LOG2E),
        grid=(s // tm, heads),
        in_specs=[
            pl.BlockSpec((tm, MLA_Q_RANK), lambda i, h: (i, cq_blk)),
            pl.BlockSpec((tm, MLA_KV_RANK), lambda i, h: (i, ckv_blk)),
            pl.BlockSpec((tm, LANES), lambda i, h: (i, 0)),
            pl.BlockSpec((tm, LANES), lambda i, h: (i, 0)),
            pl.BlockSpec((1, MLA_Q_RANK), lambda i, h: (0, 0)),
            pl.BlockSpec((1, MLA_KV_RANK), lambda i, h: (0, 0)),
            pl.BlockSpec((MLA_Q_RANK, MLA_QK_PAD), lambda i, h: (0, h)),
            pl.BlockSpec((MLA_KV_RANK, MLA_NOPE), lambda i, h: (0, h)),
            pl.BlockSpec((MLA_KV_RANK, MLA_V), lambda i, h: (0, h)),
            pl.BlockSpec((1, MLA_QK_PAD), lambda i, h: (0, 0)),
            pl.BlockSpec((1, MLA_QK_PAD), lambda i, h: (0, 0)),
        ],
        out_specs=[
            pl.BlockSpec((tm, MLA_QK_PAD), lambda i, h: (i, h)),
            pl.BlockSpec((tm, MLA_QK_PAD), lambda i, h: (i, h)),
            pl.BlockSpec((tm, MLA_V), lambda i, h: (i, h)),
        ],
        out_shape=[
            jax.ShapeDtypeStruct((s, heads * MLA_QK_PAD), BF16),
            jax.ShapeDtypeStruct((s, heads * MLA_QK_PAD), BF16),
            jax.ShapeDtypeStruct((s, heads * MLA_V), BF16),
        ],
        scratch_shapes=[
            pltpu.VMEM((tm, MLA_Q_RANK), BF16),
            pltpu.VMEM((tm, MLA_KV_RANK), BF16),
            pltpu.VMEM((tm, LANES), F32),
            pltpu.VMEM((tm, LANES), F32),
            pltpu.VMEM((tm, LANES), F32),
        ],
        compiler_params=_cparams(("parallel", "arbitrary")),
        name="mla_prep",
    )(main, main, tail, ang, gql, gkvl, wq, wk, wv, gq, gk)


def _attn_kernel(q_ref, k_ref, v_ref, o_ref, vt_ref, acc_ref, s0_ref, s1_ref, s2_ref, s3_ref, *, tq, tk, sub,
                 extend_q):
    qi = pl.program_id(1)
    s_len, dv = v_ref.shape
    nsub = tq // sub
    tr = min(512, s_len)

    @pl.when(qi == 0)
    def _():
        def transpose_block(c, carry):
            st = pl.multiple_of(c * tr, tr)
            vt_ref[:dv, pl.ds(st, tr)] = v_ref[pl.ds(st, tr), :].astype(F32).T.astype(BF16)
            return carry

        lax.fori_loop(0, s_len // tr, transpose_block, 0)
        vt_ref[dv:, :] = jnp.ones((vt_ref.shape[0] - dv, s_len), BF16)

    q = q_ref[...]
    if extend_q:
        lane = lax.broadcasted_iota(jnp.int32, (tq, LANES), 1)
        q = jnp.concatenate([q, jnp.where(lane < 3, 1.0, 0.0).astype(BF16)], axis=1)
    qs = [q[c * sub:(c + 1) * sub, :] for c in range(nsub)]
    acc_ref[...] = jnp.zeros(acc_ref.shape, F32)

    def logits(c, kstart, buf):
        k = k_ref[pl.ds(kstart, tk), :]
        s_refs[buf][...] = lax.dot_general(k, qs[c], (((1,), (1,)), ((), ())), preferred_element_type=F32)

    def finish(c, m, kstart, buf, key_minus_query):
        st = s_refs[buf][...]
        if key_minus_query is not None:
            key = lax.broadcasted_iota(jnp.int32, (tk, sub), 0)
            qry = lax.broadcasted_iota(jnp.int32, (tk, sub), 1)
            st = jnp.where(key + key_minus_query <= qry, st, -jnp.inf)
        m_new = jnp.maximum(m, jnp.max(st, axis=0, keepdims=True))
        alpha = jnp.exp2(m - m_new)
        p = jnp.exp2(st - m_new).astype(BF16)
        pv = jnp.dot(vt_ref[:, pl.ds(kstart, tk)], p, preferred_element_type=F32)
        acc_ref[c] = alpha * acc_ref[c] + pv
        return m_new

    s_refs = (s0_ref, s1_ref, s2_ref, s3_ref)
    nbuf = len(s_refs)
    ahead = 2
    span = [(c, t) for t in range(tq // tk) for c in range(nsub)]
    assert tk % sub == 0 and tq % tk == 0 and len(span) % nbuf == 0 and len(span) >= ahead
    tail = []
    for c in range(nsub):
        for t in range(tq // tk):
            if t * tk <= (c + 1) * sub - 1:
                fully_visible = (t + 1) * tk - 1 <= c * sub
                tail.append((c, t, None if fully_visible else t * tk - c * sub))
    assert [x[:2] for x in tail[:ahead]] == span[:ahead]
    for i in range(ahead):
        logits(span[i][0], span[i][1] * tk, i)

    def sweep(j, ms):
        base = pl.multiple_of(j * tq, tq)
        ms = list(ms)
        for i, (c, t) in enumerate(span):
            nc, nt = span[(i + ahead) % len(span)]
            nbase = base if i + ahead < len(span) else base + tq
            logits(nc, pl.multiple_of(nbase + nt * tk, tk), (i + ahead) % nbuf)
            ms[c] = finish(c, ms[c], pl.multiple_of(base + t * tk, tk), i % nbuf, None)
        return tuple(ms)

    ms = list(lax.fori_loop(0, qi, sweep, tuple(jnp.full((1, sub), -jnp.inf, F32) for _ in range(nsub))))
    base = pl.multiple_of(qi * tq, tq)
    for i, (c, t, off) in enumerate(tail):
        if i + ahead < len(tail):
            logits(tail[i + ahead][0], pl.multiple_of(base + tail[i + ahead][1] * tk, tk), (i + ahead) % nbuf)
        ms[c] = finish(c, ms[c], pl.multiple_of(base + t * tk, tk), i % nbuf, off)
    for c in range(nsub):
        a = acc_ref[c]
        o_ref[c * sub:(c + 1) * sub, :] = (a[:dv, :] / a[dv:dv + 1, :]).T.astype(o_ref.dtype)


def _attention(q, k, v, *, heads, dq_in, dk, dv, q_off, v_off, tq, tk, sub, name):
    s = q.shape[0]
    ones_rows = 16
    return pl.pallas_call(
        functools.partial(_attn_kernel, tq=tq, tk=tk, sub=sub, extend_q=dq_in != dk),
        grid=(heads, s // tq),
        in_specs=[
            pl.BlockSpec((tq, dq_in), lambda h, i: (i, q_off + h)),
            pl.BlockSpec((s, dk), lambda h, i: (0, h)),
            pl.BlockSpec((s, dv), lambda h, i: (0, v_off + h)),
        ],
        out_specs=pl.BlockSpec((tq, dv), lambda h, i: (i, h)),
        out_shape=jax.ShapeDtypeStruct((s, heads * dv), BF16),
        scratch_shapes=[pltpu.VMEM((dv + ones_rows, s), BF16), pltpu.VMEM((tq // sub, dv + ones_rows, sub), F32),
                        *[pltpu.VMEM((tk, sub), F32) for _ in range(4)]],
        compiler_params=_cparams(("parallel", "arbitrary")),
        name=name,
    )(q, k, v)


def _merge_kernel(ya_ref, yb_ref, ga_ref, gb_ref, bg_ref, w_ref, x_ref, o_ref):
    half = ya_ref.shape[1]
    ga = ga_ref[...].astype(F32) + bg_ref[:, :half]
    gb = gb_ref[...].astype(F32) + bg_ref[:, half:]
    ma = (ya_ref[...].astype(F32) / (1.0 + jnp.exp(-ga))).astype(BF16)
    mb = (yb_ref[...].astype(F32) / (1.0 + jnp.exp(-gb))).astype(BF16)
    acc = jnp.dot(ma, w_ref[:half, :], preferred_element_type=F32)
    acc = acc + jnp.dot(mb, w_ref[half:, :], preferred_element_type=F32)
    o_ref[...] = x_ref[...] + acc


def _merge(ya, yb, main, b_gate, w_out, x, *, tm, ga_blk, gb_blk):
    s, d = x.shape
    half = ya.shape[1]
    return pl.pallas_call(
        _merge_kernel,
        grid=(s // tm,),
        in_specs=[
            pl.BlockSpec((tm, half), lambda i: (i, 0)),
            pl.BlockSpec((tm, half), lambda i: (i, 0)),
            pl.BlockSpec((tm, half), lambda i: (i, ga_blk)),
            pl.BlockSpec((tm, half), lambda i: (i, gb_blk)),
            pl.BlockSpec((1, 2 * half), lambda i: (0, 0)),
            pl.BlockSpec((2 * half, d), lambda i: (0, 0), pipeline_mode=pl.Buffered(1)),
            pl.BlockSpec((tm, d), lambda i: (i, 0)),
        ],
        out_specs=pl.BlockSpec((tm, d), lambda i: (i, 0)),
        out_shape=jax.ShapeDtypeStruct((s, d), F32),
        compiler_params=_cparams(("parallel",)),
        name="merge",
    )(ya, yb, main, main, b_gate, w_out, x)


def _pairs():
    return [(p, q) for p in range(PEER_SORT) for q in range(PEER_SORT) if (p + 1) * (q + 1) <= PEER_SORT]


def _route_kernel(x_ref, g_ref, wq_ref, keys_ref, h2_ref, s2_ref, e2_ref, thr_ref, e1n_ref, a_ref, b_ref, s1_ref):
    x = x_ref[...]
    ms = jnp.mean(x * x, axis=-1, keepdims=True)
    h2 = (x * lax.rsqrt(ms + NORM_EPS) * g_ref[...]).astype(BF16)
    h2_ref[...] = h2.astype(F32).T.astype(BF16)
    qp = jnp.dot(h2, wq_ref[...], preferred_element_type=F32)
    neg_inf = -jnp.inf

    def scores_t(h, c):
        blk = qp[:, (2 * h + c) * PEER_HALF:(2 * h + c + 1) * PEER_HALF]
        return lax.dot_general(keys_ref[2 * h + c], blk, (((1,), (1,)), ((), ())),
                               preferred_element_type=F32, precision=lax.Precision.HIGHEST)

    def top_values(st, dst_ref, h):
        work = st
        for r in range(PEER_SORT):
            m = jnp.max(work, axis=0, keepdims=True)
            dst_ref[r, h:h + 1, :] = m
            if r + 1 < PEER_SORT:
                work = jnp.where(work == m, neg_inf, work)

    for h in range(PEER_HEADS):
        s1 = scores_t(h, 0)
        s2 = scores_t(h, 1)
        s1_ref[h] = s1
        s2_ref[h] = s2
        top_values(s1, a_ref, h)
        top_values(s2, b_ref, h)

    cands = [a_ref[p] + b_ref[q] for (p, q) in _pairs()]
    work = list(cands)
    tau = None
    for r in range(PEER_SORT):
        m = functools.reduce(jnp.maximum, work)
        if r + 1 == PEER_TOPK:
            tau = m
        if r + 1 < PEER_SORT:
            work = [jnp.where(c == m, neg_inf, c) for c in work]
    tau_mid = 0.5 * (tau + m)
    top = a_ref[0] + b_ref[0]
    z = functools.reduce(lambda u, w: u + w, [jnp.where(c >= tau, jnp.exp(c - top), 0.0) for c in cands])
    inv_z = 1.0 / z
    a0 = a_ref[0]
    b0 = b_ref[0]
    for h in range(PEER_HEADS):
        s1 = s1_ref[h]
        thr_ref[:, h, :] = tau_mid[h:h + 1, :] - s1
        e1n_ref[:, h, :] = jnp.exp(s1 - a0[h:h + 1, :]) * inv_z[h:h + 1, :]
        e2_ref[h] = jnp.exp(s2_ref[h] - b0[h:h + 1, :])


def _route(x1, g, wq, keys, *, tm):
    s, d = x1.shape
    route_shape = jax.ShapeDtypeStruct((PEER_HEADS, PEER_N_KEYS, s), F32)
    route_spec = pl.BlockSpec((PEER_HEADS, PEER_N_KEYS, tm), lambda i: (0, 0, i))
    by_key_shape = jax.ShapeDtypeStruct((PEER_N_KEYS, PEER_HEADS, s), F32)
    by_key_spec = pl.BlockSpec((PEER_N_KEYS, PEER_HEADS, tm), lambda i: (0, 0, i))
    return pl.pallas_call(
        _route_kernel,
        grid=(s // tm,),
        in_specs=[
            pl.BlockSpec((tm, d), lambda i: (i, 0)),
            pl.BlockSpec((1, d), lambda i: (0, 0)),
            pl.BlockSpec(wq.shape, lambda i: (0, 0), pipeline_mode=pl.Buffered(1)),
            pl.BlockSpec(keys.shape, lambda i: (0, 0, 0)),
        ],
        out_specs=[pl.BlockSpec((d, tm), lambda i: (0, i)), route_spec, route_spec, by_key_spec, by_key_spec],
        out_shape=[jax.ShapeDtypeStruct((d, s), BF16), route_shape, route_shape, by_key_shape, by_key_shape],
        scratch_shapes=[pltpu.VMEM((PEER_SORT, PEER_HEADS, tm), F32), pltpu.VMEM((PEER_SORT, PEER_HEADS, tm), F32),
                        pltpu.VMEM((PEER_HEADS, PEER_N_KEYS, tm), F32)],
        compiler_params=_cparams(("parallel",)),
        name="route",
    )(x1, g, wq, keys)


def _gelu(a):
    return 0.5 * a * (1.0 + lax.erf(a * (2.0 ** -0.5)))


def _peer_kernel(h2_ref, u_ref, v_ref, s2_ref, e2_ref, thr_ref, e1n_ref, x1_ref, o_ref, w0_ref, w1_ref, a0_ref,
                 a1_ref, a2_ref, *, sub):
    ei = pl.program_id(1)
    te = u_ref.shape[0]
    tb = h2_ref.shape[1]
    ni = te // PEER_N_KEYS
    nsub = tb // sub
    a_refs = (a0_ref, a1_ref, a2_ref)[:min(3, nsub)]
    ahead = len(a_refs) - 1

    @pl.when(ei == 0)
    def _():
        o_ref[...] = x1_ref[...]

    def activations(n):
        a_refs[n % len(a_refs)][...] = jnp.dot(u_ref[...], h2_ref[:, n * sub:(n + 1) * sub],
                                               preferred_element_type=F32)

    w_refs = (w0_ref, w1_ref)

    def weights(n, ii):
        a_ref = a_refs[n % len(a_refs)]
        exp_rows = slice(ii * PEER_N_KEYS, (ii + 1) * PEER_N_KEYS)
        for tt in range(sub // LANES):
            tok = slice(n * sub + tt * LANES, n * sub + (tt + 1) * LANES)
            g = jnp.zeros((PEER_N_KEYS, LANES), F32)
            for h in range(PEER_HEADS):
                thr = thr_ref[ii, h:h + 1, tok]
                e1n = e1n_ref[ii, h:h + 1, tok]
                g = g + jnp.where(s2_ref[h, :, tok] >= thr, e2_ref[h, :, tok], 0.0) * e1n
            w = g * _gelu(a_ref[exp_rows, tt * LANES:(tt + 1) * LANES])
            w_refs[n % 2][tt * LANES:(tt + 1) * LANES, exp_rows] = w.T.astype(BF16)

    def mix(n):
        rows = slice(n * sub, (n + 1) * sub)
        o_ref[rows, :] += jnp.dot(w_refs[n % 2][...], v_ref[...], preferred_element_type=F32)

    for n in range(min(ahead, nsub)):
        activations(n)
    for n in range(nsub):
        for ii in range(ni):
            if ii == 1 and n + ahead < nsub:
                activations(n + ahead)
            if ii == ni - 1 and n > 0:
                mix(n - 1)
            weights(n, ii)
    mix(nsub - 1)


def _peer(h2, u, v, s2, e2, thr, e1n, x1, *, tb, te, sub):
    s, d = x1.shape
    ne = u.shape[0]
    route_spec = pl.BlockSpec((PEER_HEADS, PEER_N_KEYS, tb), lambda i, e: (0, 0, i))
    by_key_spec = pl.BlockSpec((te // PEER_N_KEYS, PEER_HEADS, tb), lambda i, e: (e, 0, i))
    return pl.pallas_call(
        functools.partial(_peer_kernel, sub=sub),
        grid=(s // tb, ne // te),
        in_specs=[
            pl.BlockSpec((d, tb), lambda i, e: (0, i)),
            pl.BlockSpec((te, d), lambda i, e: (e, 0)),
            pl.BlockSpec((te, d), lambda i, e: (e, 0)),
            route_spec, route_spec, by_key_spec, by_key_spec,
            pl.BlockSpec((tb, d), lambda i, e: (i, 0)),
        ],
        out_specs=pl.BlockSpec((tb, d), lambda i, e: (i, 0)),
        out_shape=jax.ShapeDtypeStruct((s, d), F32),
        scratch_shapes=[pltpu.VMEM((sub, te), BF16) for _ in range(2)] + [pltpu.VMEM((te, sub), F32) for _ in range(3)],
        compiler_params=_cparams(("parallel", "arbitrary")),
        name="peer",
    )(h2, u, v, s2, e2, thr, e1n, x1)


def _blocks(s):
    return dict(
        proj_tm=min(1024, s), proj_tn=1024,
        cum_rows=min(256, s),
        prep_tm=min(1024, s), fox_prep_tm=min(512, s),
        attn_tq=min(2048, s), attn_tk=min(512, s), attn_sub=min(512, s),
        merge_tm=min(256, s),
        route_tm=min(512, s),
        peer_tb=min(512, s), peer_te=512, peer_sub=min(256, s),
    )


def kernel(x, positions, mix_norm_g, w_in, b_forget, b_gate, mla_q_latent_g, w_q_up, mla_kv_latent_g, w_kv_up,
           mla_q_norm_g, mla_k_norm_g, fox_q_norm_g, fox_k_norm_g, w_out, ffn_norm_g, w_peer_q, peer_sub_keys,
           peer_u, peer_v):
    bsz, s, d = x.shape
    assert bsz == 1 and mix_norm_g.shape[0] == 1
    blk = _blocks(s)
    half_r = MLA_ROPE // 2
    fox_w = FOX_HEADS * FOX_HEAD_DIM
    mla_w = MLA_HEADS * MLA_V

    wi = w_in[0]
    o = 0
    parts = {}
    for name, width in (("cq", MLA_Q_RANK), ("ckv", MLA_KV_RANK), ("kpe", MLA_ROPE), ("fq", fox_w), ("fk", fox_w),
                        ("fv", fox_w), ("fl", FOX_HEADS), ("ga", mla_w), ("gb", fox_w)):
        parts[name] = wi[:, o:o + width]
        o += width
    w_main = jnp.concatenate([parts[n] for n in ("fq", "fk", "fv", "ga", "gb", "cq", "ckv")], axis=1).astype(BF16)
    zcol = lambda n: jnp.zeros((d, n), F32)
    w_tail = jnp.concatenate([parts["fl"], zcol(16), parts["kpe"][:, :half_r], zcol(32), parts["kpe"][:, half_r:]],
                             axis=1).astype(BF16)
    n_main = w_main.shape[1]
    fox_scale = FOX_HEAD_DIM ** -0.5 * LOG2E
    gain_main = jnp.concatenate([jnp.tile(fox_q_norm_g[0], FOX_HEADS) * fox_scale, jnp.tile(fox_k_norm_g[0], FOX_HEADS),
                                 jnp.ones((n_main - 2 * fox_w,), F32)])[None, :]
    x2 = x[0]
    g_mix = mix_norm_g[0][None, :]

    tn = blk["proj_tn"]
    main = _proj(x2, g_mix, w_main, gain_main, tm=blk["proj_tm"], tn=tn, n_norm_tiles=2 * fox_w // tn, out_dtype=BF16)
    tail = _proj(x2, g_mix, w_tail, jnp.ones((1, LANES), F32), tm=blk["proj_tm"], tn=LANES, n_norm_tiles=0,
                 out_dtype=F32)

    b_pad = jnp.concatenate([b_forget[0], jnp.zeros((LANES - FOX_HEADS,), F32)])[None, :]
    cum = _cum(tail, b_pad, rows=blk["cum_rows"])
    k_b = _fox_prep(main, cum, tm=blk["fox_prep_tm"], k_blk=1)

    def lane_layout(a, b):
        z = jnp.zeros_like(a)
        return jnp.concatenate([z, a, z, b], axis=-1)

    wq3 = w_q_up[0].reshape(MLA_Q_RANK, MLA_HEADS, MLA_QK_DIM)
    wq_pad = jnp.concatenate([wq3[..., :MLA_NOPE],
                              lane_layout(wq3[..., MLA_NOPE:MLA_NOPE + half_r], wq3[..., MLA_NOPE + half_r:])],
                             axis=-1).reshape(MLA_Q_RANK, MLA_HEADS * MLA_QK_PAD).astype(BF16)
    wkv3 = w_kv_up[0].reshape(MLA_KV_RANK, MLA_HEADS, MLA_NOPE + MLA_V)
    wk = wkv3[..., :MLA_NOPE].reshape(MLA_KV_RANK, MLA_HEADS * MLA_NOPE).astype(BF16)
    wv = wkv3[..., MLA_NOPE:].reshape(MLA_KV_RANK, MLA_HEADS * MLA_V).astype(BF16)

    def gain_layout(gv):
        return jnp.concatenate([gv[:MLA_NOPE], lane_layout(gv[MLA_NOPE:MLA_NOPE + half_r], gv[MLA_NOPE + half_r:])])[None, :]

    inv_freq = ROPE_THETA ** (-jnp.arange(half_r, dtype=F32) / half_r)
    ang = positions[0].astype(F32)[:, None] * lane_layout(inv_freq, inv_freq)[None, :]
    q_a, k_a, v_a = _mla_prep(main, tail, ang, mla_q_latent_g[0][None, :], mla_kv_latent_g[0][None, :], wq_pad, wk, wv,
                              gain_layout(mla_q_norm_g[0]), gain_layout(mla_k_norm_g[0]), tm=blk["prep_tm"],
                              cq_blk=(n_main - MLA_Q_RANK - MLA_KV_RANK) // MLA_Q_RANK,
                              ckv_blk=(n_main - MLA_KV_RANK) // MLA_KV_RANK)

    tiles = dict(tq=blk["attn_tq"], tk=blk["attn_tk"], sub=blk["attn_sub"])
    y_a = _attention(q_a, k_a, v_a, heads=MLA_HEADS, dq_in=MLA_QK_PAD, dk=MLA_QK_PAD, dv=MLA_V, q_off=0, v_off=0,
                     name="attn_mla", **tiles)
    y_b = _attention(main, k_b, main, heads=FOX_HEADS, dq_in=FOX_HEAD_DIM, dk=2 * LANES, dv=FOX_HEAD_DIM, q_off=0,
                     v_off=2 * FOX_HEADS, name="attn_fox", **tiles)

    x1 = _merge(y_a, y_b, main, b_gate[0][None, :], w_out[0].astype(BF16), x2, tm=blk["merge_tm"],
                ga_blk=3 * fox_w // mla_w, gb_blk=(3 * fox_w + mla_w) // fox_w)

    keys = peer_sub_keys[0].reshape(PEER_HEADS * 2, PEER_N_KEYS, PEER_HALF)
    h2, s2, e2, thr, e1n = _route(x1, ffn_norm_g[0][None, :], w_peer_q[0].astype(BF16), keys, tm=blk["route_tm"])
    out = _peer(h2, peer_u[0].astype(BF16), peer_v[0].astype(BF16), s2, e2, thr, e1n, x1, tb=blk["peer_tb"],
                te=blk["peer_te"], sub=blk["peer_sub"])
    return out[None]
```

```python
import functools
import math

import jax
import jax.numpy as jnp
from jax import lax
from jax.experimental import pallas as pl
from jax.experimental.pallas import tpu as pltpu

F32 = jnp.float32
BF16 = jnp.bfloat16

LANES = 128
SUBLANES = 8
LOG2E = math.log2(math.e)
NORM_EPS = 1e-6
ROPE_THETA = 10000.0
MLA_HEADS = 16
MLA_Q_RANK = 512
MLA_KV_RANK = 512
MLA_NOPE = 128
MLA_ROPE = 64
MLA_V = 128
MLA_QK_DIM = MLA_NOPE + MLA_ROPE
MLA_QK_PAD = 2 * LANES
FOX_HEADS = 16
FOX_HEAD_DIM = 128
PEER_HEADS = 8
PEER_HALF = 128
PEER_N_KEYS = 128
PEER_TOPK = 16
PEER_SORT = PEER_TOPK + 1
VMEM_LIMIT = 56 * 1024 * 1024


def _cparams(sem, flags=None):
    return pltpu.CompilerParams(dimension_semantics=sem, vmem_limit_bytes=VMEM_LIMIT, flags=flags)


def _proj_kernel(x_ref, g_ref, w_ref, gain_ref, o_ref, hn_ref, *, n_norm_tiles):
    j = pl.program_id(1)

    @pl.when(j == 0)
    def _():
        x = x_ref[...]
        ms = jnp.mean(x * x, axis=-1, keepdims=True)
        hn_ref[...] = (x * lax.rsqrt(ms + NORM_EPS) * g_ref[...]).astype(BF16)

    acc = jnp.dot(hn_ref[...], w_ref[...], preferred_element_type=F32)
    tn = acc.shape[1]

    @pl.when(j < n_norm_tiles)
    def _():
        for c in range(tn // LANES):
            sl = slice(c * LANES, (c + 1) * LANES)
            a = acc[:, sl]
            ms = jnp.mean(a * a, axis=-1, keepdims=True)
            o_ref[:, sl] = (a * lax.rsqrt(ms + NORM_EPS) * gain_ref[:, sl]).astype(o_ref.dtype)

    @pl.when(j >= n_norm_tiles)
    def _():
        o_ref[...] = acc.astype(o_ref.dtype)


def _proj(x, g, w, gain, *, tm, tn, n_norm_tiles, out_dtype):
    s, d = x.shape
    n = w.shape[1]
    return pl.pallas_call(
        functools.partial(_proj_kernel, n_norm_tiles=n_norm_tiles),
        grid=(s // tm, n // tn),
        in_specs=[
            pl.BlockSpec((tm, d), lambda i, j: (i, 0)),
            pl.BlockSpec((1, d), lambda i, j: (0, 0)),
            pl.BlockSpec((d, tn), lambda i, j: (0, j)),
            pl.BlockSpec((1, tn), lambda i, j: (0, j)),
        ],
        out_specs=pl.BlockSpec((tm, tn), lambda i, j: (i, j)),
        out_shape=jax.ShapeDtypeStruct((s, n), out_dtype),
        scratch_shapes=[pltpu.VMEM((tm, d), BF16)],
        compiler_params=_cparams(("parallel", "arbitrary")),
        name="proj",
    )(x, g, w, gain)


def _log_sigmoid(z):
    return jnp.minimum(z, 0.0) - jnp.log1p(jnp.exp(-jnp.abs(z)))


def _cum_kernel(tail_ref, b_ref, o_ref, *, rows):
    s = tail_ref.shape[0]
    r_i = lax.broadcasted_iota(jnp.int32, (rows, rows), 0)
    c_i = lax.broadcasted_iota(jnp.int32, (rows, rows), 1)
    tri = (c_i <= r_i).astype(F32)

    def body(i, carry):
        start = pl.multiple_of(i * rows, rows)
        lf = _log_sigmoid(tail_ref[pl.ds(start, rows), :] + b_ref[...])
        c = jnp.dot(tri, lf, preferred_element_type=F32, precision=lax.Precision.HIGHEST) + carry
        o_ref[pl.ds(start, rows), :] = c * LOG2E
        return c[rows - 1:rows, :]

    lax.fori_loop(0, s // rows, body, jnp.zeros((1, LANES), F32))


def _cum(tail, b_pad, *, rows):
    s = tail.shape[0]
    return pl.pallas_call(
        functools.partial(_cum_kernel, rows=rows),
        grid=(1,),
        in_specs=[pl.BlockSpec((s, LANES), lambda i: (0, 0)), pl.BlockSpec((1, LANES), lambda i: (0, 0))],
        out_specs=pl.BlockSpec((s, LANES), lambda i: (0, 0)),
        out_shape=jax.ShapeDtypeStruct((s, LANES), F32),
        compiler_params=_cparams(("arbitrary",)),
        name="cum",
    )(tail, b_pad)


def _fox_prep_kernel(k_ref, cum_ref, o_ref):
    lane = lax.broadcasted_iota(jnp.int32, cum_ref.shape, 1)
    cum = cum_ref[...]
    for h in range(FOX_HEADS):
        neg = -jnp.sum(jnp.where(lane == h, cum, 0.0), axis=-1, keepdims=True)
        hi = neg.astype(BF16).astype(F32)
        mid = (neg - hi).astype(BF16).astype(F32)
        lo = (neg - hi - mid).astype(BF16).astype(F32)
        aug = jnp.where(lane == 0, hi, jnp.where(lane == 1, mid, jnp.where(lane == 2, lo, 0.0)))
        o_ref[:, 2 * h * LANES:(2 * h + 1) * LANES] = k_ref[:, h * LANES:(h + 1) * LANES]
        o_ref[:, (2 * h + 1) * LANES:(2 * h + 2) * LANES] = aug.astype(BF16)


def _fox_prep(main, cum, *, tm, k_blk):
    s = main.shape[0]
    width = FOX_HEADS * FOX_HEAD_DIM
    return pl.pallas_call(
        _fox_prep_kernel,
        grid=(s // tm,),
        in_specs=[pl.BlockSpec((tm, width), lambda i: (i, k_blk)),
                  pl.BlockSpec((tm, LANES), lambda i: (i, 0))],
        out_specs=pl.BlockSpec((tm, 2 * width), lambda i: (i, 0)),
        out_shape=jax.ShapeDtypeStruct((s, 2 * width), BF16),
        compiler_params=_cparams(("parallel",)),
        name="fox_prep",
    )(main, cum)


def _mla_prep_kernel(cq_ref, ckv_ref, tail_ref, ang_ref, gql_ref, gkvl_ref, wq_ref, wk_ref, wv_ref,
                     gq_ref, gk_ref, q_ref, k_ref, v_ref, cqn_ref, ckvn_ref, kpe_ref, cos_ref, sin_ref,
                     *, q_scale):
    h = pl.program_id(1)

    @pl.when(h == 0)
    def _():
        cq = cq_ref[...].astype(F32)
        ms = jnp.mean(cq * cq, axis=-1, keepdims=True)
        cqn_ref[...] = (cq * lax.rsqrt(ms + NORM_EPS) * gql_ref[...]).astype(BF16)
        ckv = ckv_ref[...].astype(F32)
        ms = jnp.mean(ckv * ckv, axis=-1, keepdims=True)
        ckvn_ref[...] = (ckv * lax.rsqrt(ms + NORM_EPS) * gkvl_ref[...]).astype(BF16)
        lane = lax.broadcasted_iota(jnp.int32, tail_ref.shape, 1)
        is_x1 = (lane >= 32) & (lane < 64)
        is_x2 = lane >= 96
        kpe_ref[...] = jnp.where(is_x1 | is_x2, tail_ref[...], 0.0)
        ang = ang_ref[...]
        cos_ref[...] = jnp.cos(ang)
        sn = jnp.sin(ang)
        sin_ref[...] = jnp.where(is_x1, -sn, sn)

    cos = cos_ref[...]
    sin_signed = sin_ref[...]
    ones = jnp.ones((LANES, LANES), BF16)
    m_i = lax.broadcasted_iota(jnp.int32, (LANES, LANES), 0)
    l_i = lax.broadcasted_iota(jnp.int32, (LANES, LANES), 1)
    swap = jnp.where(m_i == ((l_i + LANES // 2) & (LANES - 1)), 1.0, 0.0).astype(BF16)

    def row_sumsq(a):
        return jnp.dot((a * a).astype(BF16), ones, preferred_element_type=F32)

    def rope(v):
        return v * cos + jnp.dot(v.astype(BF16), swap, preferred_element_type=F32) * sin_signed

    q = jnp.dot(cqn_ref[...], wq_ref[...], preferred_element_type=F32)
    q_lo, q_hi = q[:, :LANES], q[:, LANES:]
    ms = (row_sumsq(q_lo) + row_sumsq(q_hi)) * (1.0 / MLA_QK_DIM)
    r = lax.rsqrt(ms + NORM_EPS) * q_scale
    q_ref[:, :LANES] = (q_lo * r * gq_ref[:, :LANES]).astype(BF16)
    q_ref[:, LANES:] = rope(q_hi * r * gq_ref[:, LANES:]).astype(BF16)

    kn = jnp.dot(ckvn_ref[...], wk_ref[...], preferred_element_type=F32)
    kpe = kpe_ref[...]
    r = lax.rsqrt((row_sumsq(kn) + row_sumsq(kpe)) * (1.0 / MLA_QK_DIM) + NORM_EPS)
    k_ref[:, :LANES] = (kn * r * gk_ref[:, :LANES]).astype(BF16)
    k_ref[:, LANES:] = rope(kpe * r * gk_ref[:, LANES:]).astype(BF16)

    v_ref[...] = jnp.dot(ckvn_ref[...], wv_ref[...], preferred_element_type=F32).astype(BF16)


def _mla_prep(main, tail, ang, gql, gkvl, wq, wk, wv, gq, gk, *, tm, cq_blk, ckv_blk):
    s = main.shape[0]
    heads = MLA_HEADS
    return pl.pallas_call(
        functools.partial(_mla_prep_kernel, q_scale=MLA_QK_DIM ** -0.5 * LOG2E),
        grid=(s // tm, heads),
        in_specs=[
            pl.BlockSpec((tm, MLA_Q_RANK), lambda i, h: (i, cq_blk)),
            pl.BlockSpec((tm, MLA_KV_RANK), lambda i, h: (i, ckv_blk)),
            pl.BlockSpec((tm, LANES), lambda i, h: (i, 0)),
            pl.BlockSpec((tm, LANES), lambda i, h: (i, 0)),
            pl.BlockSpec((1, MLA_Q_RANK), lambda i, h: (0, 0)),
            pl.BlockSpec((1, MLA_KV_RANK), lambda i, h: (0, 0)),
            pl.BlockSpec((MLA_Q_RANK, MLA_QK_PAD), lambda i, h: (0, h)),
            pl.BlockSpec((MLA_KV_RANK, MLA_NOPE), lambda i, h: (0, h)),
            pl.BlockSpec((MLA_KV_RANK, MLA_V), lambda i, h: (0, h)),
            pl.BlockSpec((1, MLA_QK_PAD), lambda i, h: (0, 0)),
            pl.BlockSpec((1, MLA_QK_PAD), lambda i, h: (0, 0)),
        ],
        out_specs=[
            pl.BlockSpec((tm, MLA_QK_PAD), lambda i, h: (i, h)),
            pl.BlockSpec((tm, MLA_QK_PAD), lambda i, h: (i, h)),
            pl.BlockSpec((tm, MLA_V), lambda i, h: (i, h)),
        ],
        out_shape=[
            jax.ShapeDtypeStruct((s, heads * MLA_QK_PAD), BF16),
            jax.ShapeDtypeStruct((s, heads * MLA_QK_PAD), BF16),
            jax.ShapeDtypeStruct((s, heads * MLA_V), BF16),
        ],
        scratch_shapes=[
            pltpu.VMEM((tm, MLA_Q_RANK), BF16),
            pltpu.VMEM((tm, MLA_KV_RANK), BF16),
            pltpu.VMEM((tm, LANES), F32),
            pltpu.VMEM((tm, LANES), F32),
            pltpu.VMEM((tm, LANES), F32),
        ],
        compiler_params=_cparams(("parallel", "arbitrary")),
        name="mla_prep",
    )(main, main, tail, ang, gql, gkvl, wq, wk, wv, gq, gk)


def _attn_kernel(q_ref, k_ref, v_ref, o_ref, vt_ref, acc_ref, s0_ref, s1_ref, s2_ref, s3_ref, *, tq, tk, sub,
                 extend_q):
    qi = pl.program_id(1)
    s_len, dv = v_ref.shape
    nsub = tq // sub
    tr = min(512, s_len)

    @pl.when(qi == 0)
    def _():
        def transpose_block(c, carry):
            st = pl.multiple_of(c * tr, tr)
            vt_ref[:dv, pl.ds(st, tr)] = v_ref[pl.ds(st, tr), :].astype(F32).T.astype(BF16)
            return carry

        lax.fori_loop(0, s_len // tr, transpose_block, 0)
        vt_ref[dv:, :] = jnp.ones((vt_ref.shape[0] - dv, s_len), BF16)

    q = q_ref[...]
    if extend_q:
        lane = lax.broadcasted_iota(jnp.int32, (tq, LANES), 1)
        q = jnp.concatenate([q, jnp.where(lane < 3, 1.0, 0.0).astype(BF16)], axis=1)
    qs = [q[c * sub:(c + 1) * sub, :] for c in range(nsub)]
    acc_ref[...] = jnp.zeros(acc_ref.shape, F32)

    def logits(c, kstart, buf):
        k = k_ref[pl.ds(kstart, tk), :]
        s_refs[buf][...] = lax.dot_general(k, qs[c], (((1,), (1,)), ((), ())), preferred_element_type=F32)

    def finish(c, m, kstart, buf, key_minus_query):
        st = s_refs[buf][...]
        if key_minus_query is not None:
            key = lax.broadcasted_iota(jnp.int32, (tk, sub), 0)
            qry = lax.broadcasted_iota(jnp.int32, (tk, sub), 1)
            st = jnp.where(key + key_minus_query <= qry, st, -jnp.inf)
        m_new = jnp.maximum(m, jnp.max(st, axis=0, keepdims=True))
        alpha = jnp.exp2(m - m_new)
        p = jnp.exp2(st - m_new).astype(BF16)
        pv = jnp.dot(vt_ref[:, pl.ds(kstart, tk)], p, preferred_element_type=F32)
        acc_ref[c] = alpha * acc_ref[c] + pv
        return m_new

    s_refs = (s0_ref, s1_ref, s2_ref, s3_ref)
    nbuf = len(s_refs)
    ahead = 2
    span = [(c, t) for t in range(tq // tk) for c in range(nsub)]
    assert tk % sub == 0 and tq % tk == 0 and len(span) % nbuf == 0 and len(span) >= ahead
    tail = []
    for c in range(nsub):
        for t in range(tq // tk):
            if t * tk <= (c + 1) * sub - 1:
                fully_visible = (t + 1) * tk - 1 <= c * sub
                tail.append((c, t, None if fully_visible else t * tk - c * sub))
    assert [x[:2] for x in tail[:ahead]] == span[:ahead]
    for i in range(ahead):
        logits(span[i][0], span[i][1] * tk, i)

    def sweep(j, ms):
        base = pl.multiple_of(j * tq, tq)
        ms = list(ms)
        for i, (c, t) in enumerate(span):
            nc, nt = span[(i + ahead) % len(span)]
            nbase = base if i + ahead < len(span) else base + tq
            logits(nc, pl.multiple_of(nbase + nt * tk, tk), (i + ahead) % nbuf)
            ms[c] = finish(c, ms[c], pl.multiple_of(base + t * tk, tk), i % nbuf, None)
        return tuple(ms)

    ms = list(lax.fori_loop(0, qi, sweep, tuple(jnp.full((1, sub), -jnp.inf, F32) for _ in range(nsub))))
    base = pl.multiple_of(qi * tq, tq)
    for i, (c, t, off) in enumerate(tail):
        if i + ahead < len(tail):
            logits(tail[i + ahead][0], pl.multiple_of(base + tail[i + ahead][1] * tk, tk), (i + ahead) % nbuf)
        ms[c] = finish(c, ms[c], pl.multiple_of(base + t * tk, tk), i % nbuf, off)
    for c in range(nsub):
        a = acc_ref[c]
        o_ref[c * sub:(c + 1) * sub, :] = (a[:dv, :] / a[dv:dv + 1, :]).T.astype(o_ref.dtype)


def _attention(q, k, v, *, heads, dq_in, dk, dv, q_off, v_off, tq, tk, sub, name):
    s = q.shape[0]
    ones_rows = 16
    return pl.pallas_call(
        functools.partial(_attn_kernel, tq=tq, tk=tk, sub=sub, extend_q=dq_in != dk),
        grid=(heads, s // tq),
        in_specs=[
            pl.BlockSpec((tq, dq_in), lambda h, i: (i, q_off + h)),
            pl.BlockSpec((s, dk), lambda h, i: (0, h)),
            pl.BlockSpec((s, dv), lambda h, i: (0, v_off + h)),
        ],
        out_specs=pl.BlockSpec((tq, dv), lambda h, i: (i, h)),
        out_shape=jax.ShapeDtypeStruct((s, heads * dv), BF16),
        scratch_shapes=[pltpu.VMEM((dv + ones_rows, s), BF16), pltpu.VMEM((tq // sub, dv + ones_rows, sub), F32),
                        *[pltpu.VMEM((tk, sub), F32) for _ in range(4)]],
        compiler_params=_cparams(("parallel", "arbitrary")),
        name=name,
    )(q, k, v)


def _merge_kernel(ya_ref, yb_ref, ga_ref, gb_ref, bg_ref, w_ref, x_ref, o_ref):
    half = ya_ref.shape[1]
    ga = ga_ref[...].astype(F32) + bg_ref[:, :half]
    gb = gb_ref[...].astype(F32) + bg_ref[:, half:]
    ma = (ya_ref[...].astype(F32) / (1.0 + jnp.exp(-ga))).astype(BF16)
    mb = (yb_ref[...].astype(F32) / (1.0 + jnp.exp(-gb))).astype(BF16)
    acc = jnp.dot(ma, w_ref[:half, :], preferred_element_type=F32)
    acc = acc + jnp.dot(mb, w_ref[half:, :], preferred_element_type=F32)
    o_ref[...] = x_ref[...] + acc


def _merge(ya, yb, main, b_gate, w_out, x, *, tm, ga_blk, gb_blk):
    s, d = x.shape
    half = ya.shape[1]
    return pl.pallas_call(
        _merge_kernel,
        grid=(s // tm,),
        in_specs=[
            pl.BlockSpec((tm, half), lambda i: (i, 0)),
            pl.BlockSpec((tm, half), lambda i: (i, 0)),
            pl.BlockSpec((tm, half), lambda i: (i, ga_blk)),
            pl.BlockSpec((tm, half), lambda i: (i, gb_blk)),
            pl.BlockSpec((1, 2 * half), lambda i: (0, 0)),
            pl.BlockSpec((2 * half, d), lambda i: (0, 0), pipeline_mode=pl.Buffered(1)),
            pl.BlockSpec((tm, d), lambda i: (i, 0)),
        ],
        out_specs=pl.BlockSpec((tm, d), lambda i: (i, 0)),
        out_shape=jax.ShapeDtypeStruct((s, d), F32),
        compiler_params=_cparams(("parallel",)),
        name="merge",
    )(ya, yb, main, main, b_gate, w_out, x)


def _pairs():
    return [(p, q) for p in range(PEER_SORT) for q in range(PEER_SORT) if (p + 1) * (q + 1) <= PEER_SORT]


def _sort_network(n):
    pairs = []
    p = 1
    while p < n:
        k = p
        while k >= 1:
            for j in range(k % p, n - k, 2 * k):
                for i in range(min(k, n - j - k)):
                    if (i + j) // (2 * p) == (i + j + k) // (2 * p):
                        pairs.append((i + j, i + j + k))
            k //= 2
        p *= 2
    return pairs


def _route_kernel(x_ref, g_ref, wq_ref, keys_ref, h2_ref, s2_ref, e2_ref, thr_ref, e1n_ref, a_ref, b_ref, s1_ref):
    x = x_ref[...]
    ms = jnp.mean(x * x, axis=-1, keepdims=True)
    h2 = (x * lax.rsqrt(ms + NORM_EPS) * g_ref[...]).astype(BF16)
    h2_ref[...] = h2.astype(F32).T.astype(BF16)
    qp = jnp.dot(h2, wq_ref[...], preferred_element_type=F32)
    neg_inf = -jnp.inf

    def scores_t(h, c):
        blk = qp[:, (2 * h + c) * PEER_HALF:(2 * h + c + 1) * PEER_HALF]
        return lax.dot_general(keys_ref[2 * h + c], blk, (((1,), (1,)), ((), ())),
                               preferred_element_type=F32, precision=lax.Precision.HIGHEST)

    def top_values(st, dst_ref, h):
        nrow = st.shape[0] // SUBLANES
        for lg in range(st.shape[1] // LANES):
            lanes = slice(lg * LANES, (lg + 1) * LANES)
            rows = [st[k * SUBLANES:(k + 1) * SUBLANES, lanes] for k in range(nrow)]
            for a, b in _sort_network(nrow):
                rows[a], rows[b] = jnp.maximum(rows[a], rows[b]), jnp.minimum(rows[a], rows[b])
            for r in range(PEER_SORT):
                m = jnp.max(rows[0], axis=0, keepdims=True)
                dst_ref[r, h:h + 1, lanes] = m
                left = PEER_SORT - 1 - r
                if left:
                    hit = rows[0] == m
                    for k in range(min(left, nrow)):
                        below = rows[k + 1] if k + 1 < nrow else neg_inf
                        rows[k] = jnp.where(hit, below, rows[k])

    for h in range(PEER_HEADS):
        s1 = scores_t(h, 0)
        s2 = scores_t(h, 1)
        s1_ref[h] = s1
        s2_ref[h] = s2
        top_values(s1, a_ref, h)
        top_values(s2, b_ref, h)

    cands = [a_ref[p] + b_ref[q] for (p, q) in _pairs()]
    work = list(cands)
    tau = None
    for r in range(PEER_SORT):
        m = functools.reduce(jnp.maximum, work)
        if r + 1 == PEER_TOPK:
            tau = m
        if r + 1 < PEER_SORT:
            work = [jnp.where(c == m, neg_inf, c) for c in work]
    tau_mid = 0.5 * (tau + m)
    top = a_ref[0] + b_ref[0]
    z = functools.reduce(lambda u, w: u + w, [jnp.where(c >= tau, jnp.exp(c - top), 0.0) for c in cands])
    inv_z = 1.0 / z
    a0 = a_ref[0]
    b0 = b_ref[0]
    for h in range(PEER_HEADS):
        s1 = s1_ref[h]
        thr_ref[:, h, :] = tau_mid[h:h + 1, :] - s1
        e1n_ref[:, h, :] = jnp.exp(s1 - a0[h:h + 1, :]) * inv_z[h:h + 1, :]
        e2_ref[h] = jnp.exp(s2_ref[h] - b0[h:h + 1, :])


def _route(x1, g, wq, keys, *, tm):
    s, d = x1.shape
    route_shape = jax.ShapeDtypeStruct((PEER_HEADS, PEER_N_KEYS, s), F32)
    route_spec = pl.BlockSpec((PEER_HEADS, PEER_N_KEYS, tm), lambda i: (0, 0, i))
    by_key_shape = jax.ShapeDtypeStruct((PEER_N_KEYS, PEER_HEADS, s), F32)
    by_key_spec = pl.BlockSpec((PEER_N_KEYS, PEER_HEADS, tm), lambda i: (0, 0, i))
    return pl.pallas_call(
        _route_kernel,
        grid=(s // tm,),
        in_specs=[
            pl.BlockSpec((tm, d), lambda i: (i, 0)),
            pl.BlockSpec((1, d), lambda i: (0, 0)),
            pl.BlockSpec(wq.shape, lambda i: (0, 0), pipeline_mode=pl.Buffered(1)),
            pl.BlockSpec(keys.shape, lambda i: (0, 0, 0)),
        ],
        out_specs=[pl.BlockSpec((d, tm), lambda i: (0, i)), route_spec, route_spec, by_key_spec, by_key_spec],
        out_shape=[jax.ShapeDtypeStruct((d, s), BF16), route_shape, route_shape, by_key_shape, by_key_shape],
        scratch_shapes=[pltpu.VMEM((PEER_SORT, PEER_HEADS, tm), F32), pltpu.VMEM((PEER_SORT, PEER_HEADS, tm), F32),
                        pltpu.VMEM((PEER_HEADS, PEER_N_KEYS, tm), F32)],
        compiler_params=_cparams(("parallel",)),
        name="route",
    )(x1, g, wq, keys)


def _gelu(a):
    return 0.5 * a * (1.0 + lax.erf(a * (2.0 ** -0.5)))


def _peer_kernel(h2_ref, u_ref, v_ref, s2_ref, e2_ref, thr_ref, e1n_ref, x1_ref, o_ref, w0_ref, w1_ref, a0_ref,
                 a1_ref, a2_ref, *, sub):
    ei = pl.program_id(1)
    te = u_ref.shape[0]
    tb = h2_ref.shape[1]
    ni = te // PEER_N_KEYS
    nsub = tb // sub
    a_refs = (a0_ref, a1_ref, a2_ref)[:min(3, nsub)]
    ahead = len(a_refs) - 1

    @pl.when(ei == 0)
    def _():
        o_ref[...] = x1_ref[...]

    def activations(n):
        a_refs[n % len(a_refs)][...] = jnp.dot(u_ref[...], h2_ref[:, n * sub:(n + 1) * sub],
                                               preferred_element_type=F32)

    w_refs = (w0_ref, w1_ref)

    def weights(n, ii):
        a_ref = a_refs[n % len(a_refs)]
        exp_rows = slice(ii * PEER_N_KEYS, (ii + 1) * PEER_N_KEYS)
        for tt in range(sub // LANES):
            tok = slice(n * sub + tt * LANES, n * sub + (tt + 1) * LANES)
            g = jnp.zeros((PEER_N_KEYS, LANES), F32)
            for h in range(PEER_HEADS):
                thr = thr_ref[ii, h:h + 1, tok]
                e1n = e1n_ref[ii, h:h + 1, tok]
                g = g + jnp.where(s2_ref[h, :, tok] >= thr, e2_ref[h, :, tok], 0.0) * e1n
            w = g * _gelu(a_ref[exp_rows, tt * LANES:(tt + 1) * LANES])
            w_refs[n % 2][tt * LANES:(tt + 1) * LANES, exp_rows] = w.T.astype(BF16)

    def mix(n):
        rows = slice(n * sub, (n + 1) * sub)
        o_ref[rows, :] += jnp.dot(w_refs[n % 2][...], v_ref[...], preferred_element_type=F32)

    for n in range(min(ahead, nsub)):
        activations(n)
    for n in range(nsub):
        for ii in range(ni):
            if ii == 1 and n + ahead < nsub:
                activations(n + ahead)
            if ii == ni - 1 and n > 0:
                mix(n - 1)
            weights(n, ii)
    mix(nsub - 1)


def _peer(h2, u, v, s2, e2, thr, e1n, x1, *, tb, te, sub):
    s, d = x1.shape
    ne = u.shape[0]
    route_spec = pl.BlockSpec((PEER_HEADS, PEER_N_KEYS, tb), lambda i, e: (0, 0, i))
    by_key_spec = pl.BlockSpec((te // PEER_N_KEYS, PEER_HEADS, tb), lambda i, e: (e, 0, i))
    return pl.pallas_call(
        functools.partial(_peer_kernel, sub=sub),
        grid=(s // tb, ne // te),
        in_specs=[
            pl.BlockSpec((d, tb), lambda i, e: (0, i)),
            pl.BlockSpec((te, d), lambda i, e: (e, 0)),
            pl.BlockSpec((te, d), lambda i, e: (e, 0)),
            route_spec, route_spec, by_key_spec, by_key_spec,
            pl.BlockSpec((tb, d), lambda i, e: (i, 0)),
        ],
        out_specs=pl.BlockSpec((tb, d), lambda i, e: (i, 0)),
        out_shape=jax.ShapeDtypeStruct((s, d), F32),
        scratch_shapes=[pltpu.VMEM((sub, te), BF16) for _ in range(2)] + [pltpu.VMEM((te, sub), F32) for _ in range(3)],
        compiler_params=_cparams(("parallel", "arbitrary")),
        name="peer",
    )(h2, u, v, s2, e2, thr, e1n, x1)


def _blocks(s):
    return dict(
        proj_tm=min(1024, s), proj_tn=1024,
        cum_rows=min(256, s),
        prep_tm=min(1024, s), fox_prep_tm=min(512, s),
        attn_tq=min(2048, s), attn_tk=min(512, s), attn_sub=min(512, s),
        merge_tm=min(256, s),
        route_tm=min(512, s),
        peer_tb=min(512, s), peer_te=512, peer_sub=min(256, s),
    )


def kernel(x, positions, mix_norm_g, w_in, b_forget, b_gate, mla_q_latent_g, w_q_up, mla_kv_latent_g, w_kv_up,
           mla_q_norm_g, mla_k_norm_g, fox_q_norm_g, fox_k_norm_g, w_out, ffn_norm_g, w_peer_q, peer_sub_keys,
           peer_u, peer_v):
    bsz, s, d = x.shape
    assert bsz == 1 and mix_norm_g.shape[0] == 1
    blk = _blocks(s)
    half_r = MLA_ROPE // 2
    fox_w = FOX_HEADS * FOX_HEAD_DIM
    mla_w = MLA_HEADS * MLA_V

    wi = w_in[0]
    o = 0
    parts = {}
    for name, width in (("cq", MLA_Q_RANK), ("ckv", MLA_KV_RANK), ("kpe", MLA_ROPE), ("fq", fox_w), ("fk", fox_w),
                        ("fv", fox_w), ("fl", FOX_HEADS), ("ga", mla_w), ("gb", fox_w)):
        parts[name] = wi[:, o:o + width]
        o += width
    w_main = jnp.concatenate([parts[n] for n in ("fq", "fk", "fv", "ga", "gb", "cq", "ckv")], axis=1).astype(BF16)
    zcol = lambda n: jnp.zeros((d, n), F32)
    w_tail = jnp.concatenate([parts["fl"], zcol(16), parts["kpe"][:, :half_r], zcol(32), parts["kpe"][:, half_r:]],
                             axis=1).astype(BF16)
    n_main = w_main.shape[1]
    fox_scale = FOX_HEAD_DIM ** -0.5 * LOG2E
    gain_main = jnp.concatenate([jnp.tile(fox_q_norm_g[0], FOX_HEADS) * fox_scale, jnp.tile(fox_k_norm_g[0], FOX_HEADS),
                                 jnp.ones((n_main - 2 * fox_w,), F32)])[None, :]
    x2 = x[0]
    g_mix = mix_norm_g[0][None, :]

    tn = blk["proj_tn"]
    main = _proj(x2, g_mix, w_main, gain_main, tm=blk["proj_tm"], tn=tn, n_norm_tiles=2 * fox_w // tn, out_dtype=BF16)
    tail = _proj(x2, g_mix, w_tail, jnp.ones((1, LANES), F32), tm=blk["proj_tm"], tn=LANES, n_norm_tiles=0,
                 out_dtype=F32)

    b_pad = jnp.concatenate([b_forget[0], jnp.zeros((LANES - FOX_HEADS,), F32)])[None, :]
    cum = _cum(tail, b_pad, rows=blk["cum_rows"])
    k_b = _fox_prep(main, cum, tm=blk["fox_prep_tm"], k_blk=1)

    def lane_layout(a, b):
        z = jnp.zeros_like(a)
        return jnp.concatenate([z, a, z, b], axis=-1)

    wq3 = w_q_up[0].reshape(MLA_Q_RANK, MLA_HEADS, MLA_QK_DIM)
    wq_pad = jnp.concatenate([wq3[..., :MLA_NOPE],
                              lane_layout(wq3[..., MLA_NOPE:MLA_NOPE + half_r], wq3[..., MLA_NOPE + half_r:])],
                             axis=-1).reshape(MLA_Q_RANK, MLA_HEADS * MLA_QK_PAD).astype(BF16)
    wkv3 = w_kv_up[0].reshape(MLA_KV_RANK, MLA_HEADS, MLA_NOPE + MLA_V)
    wk = wkv3[..., :MLA_NOPE].reshape(MLA_KV_RANK, MLA_HEADS * MLA_NOPE).astype(BF16)
    wv = wkv3[..., MLA_NOPE:].reshape(MLA_KV_RANK, MLA_HEADS * MLA_V).astype(BF16)

    def gain_layout(gv):
        return jnp.concatenate([gv[:MLA_NOPE], lane_layout(gv[MLA_NOPE:MLA_NOPE + half_r], gv[MLA_NOPE + half_r:])])[None, :]

    inv_freq = ROPE_THETA ** (-jnp.arange(half_r, dtype=F32) / half_r)
    ang = positions[0].astype(F32)[:, None] * lane_layout(inv_freq, inv_freq)[None, :]
    q_a, k_a, v_a = _mla_prep(main, tail, ang, mla_q_latent_g[0][None, :], mla_kv_latent_g[0][None, :], wq_pad, wk, wv,
                              gain_layout(mla_q_norm_g[0]), gain_layout(mla_k_norm_g[0]), tm=blk["prep_tm"],
                              cq_blk=(n_main - MLA_Q_RANK - MLA_KV_RANK) // MLA_Q_RANK,
                              ckv_blk=(n_main - MLA_KV_RANK) // MLA_KV_RANK)

    tiles = dict(tq=blk["attn_tq"], tk=blk["attn_tk"], sub=blk["attn_sub"])
    y_a = _attention(q_a, k_a, v_a, heads=MLA_HEADS, dq_in=MLA_QK_PAD, dk=MLA_QK_PAD, dv=MLA_V, q_off=0, v_off=0,
                     name="attn_mla", **tiles)
    y_b = _attention(main, k_b, main, heads=FOX_HEADS, dq_in=FOX_HEAD_DIM, dk=2 * LANES, dv=FOX_HEAD_DIM, q_off=0,
                     v_off=2 * FOX_HEADS, name="attn_fox", **tiles)

    x1 = _merge(y_a, y_b, main, b_gate[0][None, :], w_out[0].astype(BF16), x2, tm=blk["merge_tm"],
                ga_blk=3 * fox_w // mla_w, gb_blk=(3 * fox_w + mla_w) // fox_w)

    keys = peer_sub_keys[0].reshape(PEER_HEADS * 2, PEER_N_KEYS, PEER_HALF)
    h2, s2, e2, thr, e1n = _route(x1, ffn_norm_g[0][None, :], w_peer_q[0].astype(BF16), keys, tm=blk["route_tm"])
    out = _peer(h2, peer_u[0].astype(BF16), peer_v[0].astype(BF16), s2, e2, thr, e1n, x1, tb=blk["peer_tb"],
                te=blk["peer_te"], sub=blk["peer_sub"])
    return out[None]
```
